```python
import math
import jax
import jax.numpy as jnp
from jax import lax
import numpy as np

D_MODEL = 2048
BATCH = 16
SEQ = 256
DEPTH = 4
DEC_BATCH = 2
DEC_SEQ = 2048
PAST_LEN = 512

GRID_W = 64
N_MIXERS = 3
N_LAYERS_A = (DEPTH + 2) // 3
N_LAYERS_B = (DEPTH + 1) // 3
N_LAYERS_C = DEPTH // 3
H_A = 8
DK_A = D_MODEL // (2 * H_A)
DV_A = 2 * DK_A
DH_B = 64
H_B = D_MODEL // DH_B
KV_B = 4
G_B = H_B // KV_B
WINDOW = 128
POOL_SIZES = (2, 4, 8, 16)
N_POOL_GROUPS = len(POOL_SIZES)
POOL_GROUP = D_MODEL // N_POOL_GROUPS
N_EXPERTS = 32
TOP_K = 4
D_FF = D_MODEL
SWIGLU_ALPHA = 1.702
SWIGLU_LIMIT = 7.0
EXPERT_BLOCK = 128
Q_BLOCK = 128
ROPE_BASE = 10000.0
LN_EPS = 1e-5
NEG_INF = -1e30
DEEPNORM_ALPHA = (2 * DEPTH) ** 0.25
DEEPNORM_BETA = (8 * DEPTH) ** -0.25
N_ADA = 6

kernel_name = 'hybrid_dit_diffattn_window_pool_moe_step'


def layer_norm(x, g, b):
    xf = x.astype(jnp.float32)
    mu = jnp.mean(xf, axis=-1, keepdims=True)
    var = jnp.mean(jnp.square(xf - mu), axis=-1, keepdims=True)
    y = (xf - mu) * lax.rsqrt(var + LN_EPS)
    return (y * g.astype(jnp.float32) + b.astype(jnp.float32)).astype(x.dtype)


def rms_norm(x, g):
    xf = x.astype(jnp.float32)
    y = xf * lax.rsqrt(jnp.mean(jnp.square(xf), axis=-1, keepdims=True) + LN_EPS)
    return (y * g.astype(jnp.float32)).astype(x.dtype)


def post_norm(x, sub, g, b):
    return layer_norm(DEEPNORM_ALPHA * x + sub, g, b)


def modulate(x, shift, scale):
    return x * (1.0 + scale) + shift


def ada_terms(cvec, w, b):
    return jnp.split(jax.nn.silu(cvec) @ w + b, N_ADA, axis=-1)


def rope_tables(n, head_dim):
    rows = n // GRID_W
    row = jnp.repeat(jnp.arange(rows), GRID_W).astype(jnp.float32)
    col = jnp.tile(jnp.arange(GRID_W), rows).astype(jnp.float32)
    quarter = head_dim // 4
    inv = ROPE_BASE ** (-jnp.arange(quarter, dtype=jnp.float32) / quarter)
    ang_r = row[:, None] * inv
    ang_c = col[:, None] * inv
    return (jnp.cos(ang_r), jnp.sin(ang_r), jnp.cos(ang_c), jnp.sin(ang_c))


def _rope_1d(x, cos, sin):
    half = x.shape[-1] // 2
    x1, x2 = x[..., :half], x[..., half:]
    cs = cos[None, :, None, :]
    sn = sin[None, :, None, :]
    return jnp.concatenate([x1 * cs - x2 * sn, x2 * cs + x1 * sn], axis=-1).astype(x.dtype)


def rope_2d(x, tabs):
    half = x.shape[-1] // 2
    return jnp.concatenate([_rope_1d(x[..., :half], tabs[0], tabs[1]),
                            _rope_1d(x[..., half:], tabs[2], tabs[3])], axis=-1)


def softmax_with_sink(s, sink):
    m = jnp.maximum(jnp.max(s, axis=-1, keepdims=True), sink)
    e = jnp.exp(s - m)
    return e / (jnp.sum(e, axis=-1, keepdims=True) + jnp.exp(sink - m))


def to_qblocks(x):
    b, n = x.shape[:2]
    x = x.reshape((b, n // Q_BLOCK, Q_BLOCK) + x.shape[2:])
    return jnp.moveaxis(x, 1, 0)


def from_qblocks(y):
    nb, b, qb = y.shape[:3]
    return jnp.moveaxis(y, 0, 1).reshape((b, nb * qb) + y.shape[3:])


def diff_project(h, wq, wk, wv):
    b, n, _ = h.shape
    q = (h @ wq).reshape(b, n, H_A, 2, DK_A)
    k = (h @ wk).reshape(b, n, H_A, 2, DK_A)
    v = (h @ wv).reshape(b, n, H_A, DV_A)
    return q, k, v


def diff_rope(x, tabs):
    b, n = x.shape[:2]
    return rope_2d(x.reshape(b, n, H_A * 2, DK_A), tabs).reshape(b, n, H_A, 2, DK_A)


def diff_lambda_value(lp, lam_init):
    lp = lp.astype(jnp.float32)
    return jnp.exp(jnp.sum(lp[0] * lp[1])) - jnp.exp(jnp.sum(lp[2] * lp[3])) + lam_init


def diff_core(q, k, v, lam):
    scale = DK_A ** -0.5

    def block(qb):
        s = jnp.einsum('bqhcd,bkhcd->bhcqk', qb, k).astype(jnp.float32) * scale
        p = jax.nn.softmax(s, axis=-1)
        a = p[:, :, 0] - lam * p[:, :, 1]
        return jnp.einsum('bhqk,bkhd->bqhd', a.astype(v.dtype), v)

    return from_qblocks(lax.map(block, to_qblocks(q)))


def diff_output(o, lam_init, subln_g, wo):
    b, n = o.shape[:2]
    o = rms_norm(o, subln_g) * (1.0 - lam_init)
    return o.reshape(b, n, H_A * DV_A) @ wo


def win_project(h, wq, wk, wv):
    b, n, _ = h.shape
    q = (h @ wq).reshape(b, n, H_B, DH_B)
    k = (h @ wk).reshape(b, n, KV_B, DH_B)
    v = (h @ wv).reshape(b, n, KV_B, DH_B)
    return q, k, v


def gqa_ctx_core(q, k, v, sink):
    b, n = q.shape[:2]
    qg = q.reshape(b, n, KV_B, G_B, DH_B)
    sk = sink.reshape(KV_B, G_B)[None, :, :, None, None].astype(jnp.float32)
    scale = DH_B ** -0.5

    def block(qb):
        s = jnp.einsum('bqkgd,bjkd->bkgqj', qb, k).astype(jnp.float32) * scale
        p = softmax_with_sink(s, sk)
        return jnp.einsum('bkgqj,bjkd->bqkgd', p.astype(v.dtype), v)

    o = from_qblocks(lax.map(block, to_qblocks(qg)))
    return o.reshape(b, n, H_B * DH_B)


def window_latent_core(q, k_lat, v_lat, k_ctx, v_ctx, sink):
    b, n = q.shape[:2]
    n_ctx = k_ctx.shape[1]
    qg = q.reshape(b, n, KV_B, G_B, DH_B)
    pad = ((0, 0), (WINDOW, WINDOW), (0, 0), (0, 0))
    kp = jnp.pad(k_lat, pad)
    vp = jnp.pad(v_lat, pad)
    sk = sink.reshape(KV_B, G_B)[None, :, :, None, None].astype(jnp.float32)
    scale = DH_B ** -0.5
    band = Q_BLOCK + 2 * WINDOW

    def block(args):
        i, qb = args
        kb = lax.dynamic_slice_in_dim(kp, i * Q_BLOCK, band, axis=1)
        vb = lax.dynamic_slice_in_dim(vp, i * Q_BLOCK, band, axis=1)
        qpos = i * Q_BLOCK + jnp.arange(Q_BLOCK)
        kpos = i * Q_BLOCK - WINDOW + jnp.arange(band)
        valid = ((jnp.abs(qpos[:, None] - kpos[None, :]) <= WINDOW)
                 & (kpos >= 0)[None, :] & (kpos < n)[None, :])
        s_ctx = jnp.einsum('bqkgd,bjkd->bkgqj', qb, k_ctx).astype(jnp.float32) * scale
        s_lat = jnp.einsum('bqkgd,bjkd->bkgqj', qb, kb).astype(jnp.float32) * scale
        s_lat = jnp.where(valid, s_lat, NEG_INF)
        p = softmax_with_sink(jnp.concatenate([s_ctx, s_lat], axis=-1), sk).astype(v_lat.dtype)
        return (jnp.einsum('bkgqj,bjkd->bqkgd', p[..., :n_ctx], v_ctx)
                + jnp.einsum('bkgqj,bjkd->bqkgd', p[..., n_ctx:], vb))

    o = from_qblocks(lax.map(block, (jnp.arange(n // Q_BLOCK), to_qblocks(qg))))
    return o.reshape(b, n, H_B * DH_B)


def pool_mixer(h, w, bias, ls):
    b, n, d = h.shape
    hg = h.reshape(b, n, N_POOL_GROUPS, POOL_GROUP)
    cs = jnp.concatenate([jnp.zeros((b, 1, N_POOL_GROUPS, POOL_GROUP), jnp.float32),
                          jnp.cumsum(hg.astype(jnp.float32), axis=1)], axis=1)
    t = jnp.arange(n)
    pooled = []
    for g, wsz in enumerate(POOL_SIZES):
        lo = jnp.maximum(t - wsz // 2, 0)
        hi = jnp.minimum(t + wsz // 2 - 1, n - 1)
        cnt = (hi - lo + 1).astype(jnp.float32)[None, :, None]
        pooled.append((cs[:, hi + 1, g] - cs[:, lo, g]) / cnt)
    pooled = jnp.stack(pooled, axis=2)
    diff = (pooled - hg.astype(jnp.float32)).astype(h.dtype)
    y = jnp.einsum('bngc,gce->bnge', diff, w) + bias
    return y.reshape(b, n, d) * ls


def moe(h, w_router, b_router, w_gu, b_gu, w_down, b_down):
    b, n, d = h.shape
    t = b * n
    xt = h.reshape(t, d)
    logits = (xt @ w_router + b_router).astype(jnp.float32)
    top_v, top_e = lax.top_k(logits, TOP_K)
    gates = jax.nn.softmax(top_v, axis=-1)
    flat_e = top_e.reshape(-1)
    flat_tok = jnp.repeat(jnp.arange(t, dtype=jnp.int32), TOP_K)
    order = jnp.argsort(flat_e)
    se = flat_e[order]
    st = flat_tok[order]
    sg = gates.reshape(-1)[order]
    counts = jnp.bincount(flat_e, length=N_EXPERTS)
    padded = (counts + EXPERT_BLOCK - 1) // EXPERT_BLOCK * EXPERT_BLOCK
    pend = jnp.cumsum(padded)
    pstart = pend - padded
    start = jnp.cumsum(counts) - counts
    dest = pstart[se] + jnp.arange(t * TOP_K) - start[se]
    n_blocks = -(-(t * TOP_K) // EXPERT_BLOCK) + N_EXPERTS
    slot_tok = jnp.full((n_blocks * EXPERT_BLOCK,), t, jnp.int32).at[dest].set(st)
    block_e = jnp.minimum(jnp.searchsorted(pend, jnp.arange(n_blocks) * EXPERT_BLOCK, side='right'),
                          N_EXPERTS - 1)
    x_pad = jnp.concatenate([xt, jnp.zeros((1, d), xt.dtype)], axis=0)
    xb = x_pad[slot_tok].reshape(n_blocks, EXPERT_BLOCK, d)

    def expert_block(args):
        xblk, e = args
        gu = xblk @ w_gu[e] + b_gu[e]
        gate = jnp.minimum(gu[:, :D_FF], SWIGLU_LIMIT)
        up = jnp.clip(gu[:, D_FF:], -SWIGLU_LIMIT, SWIGLU_LIMIT)
        act = (up + 1.0) * (gate * jax.nn.sigmoid(SWIGLU_ALPHA * gate))
        return act @ w_down[e] + b_down[e]

    yb = lax.map(expert_block, (xb, block_e)).reshape(-1, d)
    contrib = yb[dest] * sg[:, None].astype(yb.dtype)
    return jax.ops.segment_sum(contrib, st, num_segments=t).reshape(b, n, d)


def setup_inputs(seed: int = 0) -> dict:
    key = jax.random.key(seed)
    ks = iter(jax.random.split(key, 40))
    f32 = jnp.float32
    D = D_MODEL
    beta = DEEPNORM_BETA

    def nrm(shape, scale=1.0):
        return jax.random.normal(next(ks), shape, f32) * scale

    return {
        'x_prompt': nrm((BATCH, SEQ, D)),
        'x_sample': nrm((DEC_BATCH, DEC_SEQ, D)),
        'cache_diff_k': nrm((DEC_BATCH, N_LAYERS_A, PAST_LEN, H_A, 2, DK_A)),
        'cache_diff_v': nrm((DEC_BATCH, N_LAYERS_A, PAST_LEN, H_A, DV_A), beta),
        'cache_win_k': nrm((DEC_BATCH, N_LAYERS_B, PAST_LEN, KV_B, DH_B)),
        'cache_win_v': nrm((DEC_BATCH, N_LAYERS_B, PAST_LEN, KV_B, DH_B), beta),
        'c': nrm((DEC_BATCH, D)),
        'c_ctx': nrm((D,)),
        'w_ada': nrm((DEPTH, D, N_ADA * D), 0.5 * D ** -0.5),
        'b_ada': nrm((DEPTH, N_ADA * D), 0.02),
        'ln_g': 1.0 + nrm((DEPTH, 2, D), 0.02),
        'ln_b': nrm((DEPTH, 2, D), 0.02),
        'diff_wq': nrm((N_LAYERS_A, D, H_A * 2 * DK_A), D ** -0.5),
        'diff_wk': nrm((N_LAYERS_A, D, H_A * 2 * DK_A), D ** -0.5),
        'diff_wv': nrm((N_LAYERS_A, D, H_A * DV_A), beta * D ** -0.5),
        'diff_wo': nrm((N_LAYERS_A, H_A * DV_A, D), beta * (H_A * DV_A) ** -0.5),
        'diff_lambda': nrm((N_LAYERS_A, 4, DK_A), 0.1),
        'diff_subln_g': 1.0 + nrm((N_LAYERS_A, DV_A), 0.02),
        'win_wq': nrm((N_LAYERS_B, D, H_B * DH_B), D ** -0.5),
        'win_wk': nrm((N_LAYERS_B, D, KV_B * DH_B), D ** -0.5),
        'win_wv': nrm((N_LAYERS_B, D, KV_B * DH_B), beta * D ** -0.5),
        'win_wo': nrm((N_LAYERS_B, H_B * DH_B, D), beta * (H_B * DH_B) ** -0.5),
        'win_sink': nrm((N_LAYERS_B, H_B)),
        'pool_w': nrm((N_LAYERS_C, N_POOL_GROUPS, POOL_GROUP, POOL_GROUP), beta * POOL_GROUP ** -0.5),
        'pool_b': nrm((N_LAYERS_C, N_POOL_GROUPS, POOL_GROUP), 0.02),
        'pool_scale': 1.0 + nrm((N_LAYERS_C, D), 0.1),
        'moe_w_router': nrm((DEPTH, D, N_EXPERTS), D ** -0.5),
        'moe_b_router': nrm((DEPTH, N_EXPERTS), 0.01),
        'moe_w_gu': nrm((DEPTH, N_EXPERTS, D, 2 * D_FF), D ** -0.5),
        'moe_b_gu': nrm((DEPTH, N_EXPERTS, 2 * D_FF), 0.02),
        'moe_w_down': nrm((DEPTH, N_EXPERTS, D_FF, D), beta * D_FF ** -0.5),
        'moe_b_down': nrm((DEPTH, N_EXPERTS, D), 0.02),
    }


def reference(x_prompt, x_sample, cache_diff_k, cache_diff_v, cache_win_k, cache_win_v, c, c_ctx,
              w_ada, b_ada, ln_g, ln_b,
              diff_wq, diff_wk, diff_wv, diff_wo, diff_lambda, diff_subln_g,
              win_wq, win_wk, win_wv, win_wo, win_sink,
              pool_w, pool_b, pool_scale,
              moe_w_router, moe_b_router, moe_w_gu, moe_b_gu, moe_w_down, moe_b_down):
    n_lat = x_sample.shape[1]
    tabs_a = rope_tables(n_lat, DK_A)
    tabs_b = rope_tables(n_lat, DH_B)
    yp, ys = x_prompt, x_sample
    st_dk, st_dv, st_wk, st_wv = [], [], [], []
    ia = ib = ic = 0
    for i in range(DEPTH):
        mp = ada_terms(c_ctx, w_ada[i], b_ada[i])
        ms = [m[:, None, :] for m in ada_terms(c, w_ada[i], b_ada[i])]
        hp = modulate(yp, mp[0], mp[1])
        hs = modulate(ys, ms[0], ms[1])
        kind = i % N_MIXERS
        if kind == 0:
            lam_init = 0.8 - 0.6 * math.exp(-0.3 * i)
            lam = diff_lambda_value(diff_lambda[ia], lam_init)
            qp, kp, vp = diff_project(hp, diff_wq[ia], diff_wk[ia], diff_wv[ia])
            op = diff_core(qp, kp, vp, lam)
            qs, ks_, vs = diff_project(hs, diff_wq[ia], diff_wk[ia], diff_wv[ia])
            k_all = jnp.concatenate([cache_diff_k[:, ia], diff_rope(ks_, tabs_a)], axis=1)
            v_all = jnp.concatenate([cache_diff_v[:, ia], vs], axis=1)
            os_ = diff_core(diff_rope(qs, tabs_a), k_all, v_all, lam)
            sub_p = diff_output(op, lam_init, diff_subln_g[ia], diff_wo[ia])
            sub_s = diff_output(os_, lam_init, diff_subln_g[ia], diff_wo[ia])
            st_dk.append(kp)
            st_dv.append(vp)
            ia += 1
        elif kind == 1:
            qp, kp, vp = win_project(hp, win_wq[ib], win_wk[ib], win_wv[ib])
            sub_p = gqa_ctx_core(qp, kp, vp, win_sink[ib]) @ win_wo[ib]
            qs, ks_, vs = win_project(hs, win_wq[ib], win_wk[ib], win_wv[ib])
            os_ = window_latent_core(rope_2d(qs, tabs_b), rope_2d(ks_, tabs_b), vs,
                                     cache_win_k[:, ib], cache_win_v[:, ib], win_sink[ib])
            sub_s = os_ @ win_wo[ib]
            st_wk.append(kp)
            st_wv.append(vp)
            ib += 1
        else:
            sub_p = pool_mixer(hp, pool_w[ic], pool_b[ic], pool_scale[ic])
            sub_s = pool_mixer(hs, pool_w[ic], pool_b[ic], pool_scale[ic])
            ic += 1
        yp = post_norm(yp, mp[2] * sub_p, ln_g[i, 0], ln_b[i, 0])
        ys = post_norm(ys, ms[2] * sub_s, ln_g[i, 0], ln_b[i, 0])
        hp = modulate(yp, mp[3], mp[4])
        hs = modulate(ys, ms[3], ms[4])
        e_args = (moe_w_router[i], moe_b_router[i], moe_w_gu[i], moe_b_gu[i], moe_w_down[i], moe_b_down[i])
        yp = post_norm(yp, mp[5] * moe(hp, *e_args), ln_g[i, 1], ln_b[i, 1])
        ys = post_norm(ys, ms[5] * moe(hs, *e_args), ln_g[i, 1], ln_b[i, 1])
    new_diff_k = jnp.stack(st_dk, axis=1)
    new_diff_v = jnp.stack(st_dv, axis=1)
    new_win_k = jnp.stack(st_wk, axis=1)
    new_win_v = jnp.stack(st_wv, axis=1)
    return (yp, ys, new_diff_k, new_diff_v, new_win_k, new_win_v)
```

```python
import functools
import math

import jax
import jax.numpy as jnp
from jax import lax
from jax.experimental import pallas as pl
from jax.experimental.pallas import tpu as pltpu

D_MODEL = 2048
BATCH = 16
SEQ = 256
DEPTH = 4
DEC_BATCH = 2
DEC_SEQ = 2048
PAST_LEN = 512
GRID_W = 64
N_MIXERS = 3
H_A = 8
DK_A = D_MODEL // (2 * H_A)
DV_A = 2 * DK_A
DH_B = 64
H_B = D_MODEL // DH_B
KV_B = 4
G_B = H_B // KV_B
WINDOW = 128
POOL_SIZES = (2, 4, 8, 16)
POOL_GROUP = D_MODEL // len(POOL_SIZES)
N_EXPERTS = 32
TOP_K = 4
D_FF = D_MODEL
SWIGLU_ALPHA = 1.702
SWIGLU_LIMIT = 7.0
ROPE_BASE = 10000.0
LN_EPS = 1e-5
DEEPNORM_ALPHA = (2 * DEPTH) ** 0.25
N_ADA = 6

T_P = BATCH * SEQ
T_S = DEC_BATCH * DEC_SEQ
T_ALL = T_P + T_S
N_COND = 1 + DEC_BATCH
COND_ROWS = 8

LANES = 128
VMEM_LIMIT = 52 * 1024 * 1024
MOE_CHUNK = 256
MOE_TMAX = 1024
MOE_TF = 256
N_SLOTS = T_ALL * TOP_K + N_EXPERTS * MOE_CHUNK
N_SUPER = N_SLOTS // MOE_TMAX + N_EXPERTS
DMA_BATCH = 64


def _cparams(sem, vmem=VMEM_LIMIT):
    return pltpu.CompilerParams(dimension_semantics=sem, vmem_limit_bytes=vmem)


def _cond_group(row):
    return jnp.maximum(row // DEC_SEQ - (T_P // DEC_SEQ - 1), 0)


def _ada_kernel(c_ref, w_ref, b_ref, o_ref):
    cv = c_ref[...]
    s = (cv / (1.0 + jnp.exp(-cv))).astype(jnp.bfloat16)
    o_ref[...] = jnp.dot(s, w_ref[...].astype(jnp.bfloat16),
                         preferred_element_type=jnp.float32) + b_ref[...]


def _ada_all(cvec, w_ada, b_ada):
    tn = 1024
    n = N_ADA * D_MODEL
    return pl.pallas_call(
        _ada_kernel,
        out_shape=jax.ShapeDtypeStruct((DEPTH, COND_ROWS, n), jnp.float32),
        grid=(DEPTH, n // tn),
        in_specs=[
            pl.BlockSpec((COND_ROWS, D_MODEL), lambda l, j: (0, 0)),
            pl.BlockSpec((None, D_MODEL, tn), lambda l, j: (l, 0, j)),
            pl.BlockSpec((None, 1, tn), lambda l, j: (l, 0, j)),
        ],
        out_specs=pl.BlockSpec((None, COND_ROWS, tn), lambda l, j: (l, 0, j)),
        compiler_params=_cparams(("arbitrary", "arbitrary")),
        name="ada",
    )(cvec, w_ada, b_ada.reshape(DEPTH, 1, n))


def _modnorm_kernel(*refs, n_sub, has_gates, do_norm, emit_h, gate_row, shift_row, scale_row):
    it = iter(refs)
    x_ref = next(it)
    sub_refs = [next(it) for _ in range(n_sub)]
    gates_ref = next(it) if has_gates else None
    mod_ref = next(it)
    modn_ref = next(it)
    lng_ref = next(it)
    lnb_ref = next(it)
    xo_ref = next(it) if do_norm else None
    h_ref = next(it) if emit_h else None

    x = x_ref[...]
    if do_norm:
        if has_gates:
            gts = gates_ref[...]
            sub = sub_refs[0][...] * gts[:, 0:1]
            for k in range(1, n_sub):
                sub = sub + sub_refs[k][...] * gts[:, k:k + 1]
        else:
            sub = sub_refs[0][...]
        xf = DEEPNORM_ALPHA * x + mod_ref[gate_row:gate_row + 1, :] * sub
        mu = jnp.mean(xf, axis=-1, keepdims=True)
        xc = xf - mu
        var = jnp.mean(xc * xc, axis=-1, keepdims=True)
        x = xc * lax.rsqrt(var + LN_EPS) * lng_ref[...] + lnb_ref[...]
        xo_ref[...] = x
    if emit_h:
        h = x * (1.0 + modn_ref[scale_row:scale_row + 1, :]) + modn_ref[shift_row:shift_row + 1, :]
        h_ref[...] = h.astype(h_ref.dtype)


def _modnorm(x, subs, gates, ada4, layer, layer_next, ln_g, ln_b, ln_idx, *,
             do_norm, emit_h, h_dtype=jnp.bfloat16, gate_row=0, shift_row=0, scale_row=0):
    tm = 256
    n_sub = len(subs)
    has_gates = gates is not None
    row_spec = pl.BlockSpec((tm, D_MODEL), lambda i: (i, 0))
    in_specs = [row_spec]
    args = [x]
    if has_gates:
        for k in range(TOP_K):
            in_specs.append(pl.BlockSpec((None, tm, D_MODEL), lambda i, k=k: (k, i, 0)))
            args.append(subs[0])
        n_sub = TOP_K
        in_specs.append(pl.BlockSpec((tm, TOP_K), lambda i: (i, 0)))
        args.append(gates)
    else:
        for s in subs:
            in_specs.append(row_spec)
            args.append(s)
    in_specs.append(pl.BlockSpec((None, None, N_ADA, D_MODEL),
                                 lambda i: (layer, _cond_group(i * tm), 0, 0)))
    in_specs.append(pl.BlockSpec((None, None, N_ADA, D_MODEL),
                                 lambda i: (layer_next, _cond_group(i * tm), 0, 0)))
    args += [ada4, ada4]
    in_specs.append(pl.BlockSpec((None, None, 1, D_MODEL), lambda i: (ln_idx[0], ln_idx[1], 0, 0)))
    in_specs.append(pl.BlockSpec((None, None, 1, D_MODEL), lambda i: (ln_idx[0], ln_idx[1], 0, 0)))
    args += [ln_g.reshape(DEPTH, 2, 1, D_MODEL), ln_b.reshape(DEPTH, 2, 1, D_MODEL)]
    out_shape, out_specs = [], []
    if do_norm:
        out_shape.append(jax.ShapeDtypeStruct((T_ALL, D_MODEL), jnp.float32))
        out_specs.append(row_spec)
    if emit_h:
        out_shape.append(jax.ShapeDtypeStruct((T_ALL, D_MODEL), h_dtype))
        out_specs.append(row_spec)
    kern = functools.partial(_modnorm_kernel, n_sub=n_sub, has_gates=has_gates, do_norm=do_norm,
                             emit_h=emit_h, gate_row=gate_row, shift_row=shift_row,
                             scale_row=scale_row)
    outs = pl.pallas_call(
        kern, out_shape=out_shape, grid=(T_ALL // tm,), in_specs=in_specs, out_specs=out_specs,
        compiler_params=_cparams(("arbitrary",)), name="modnorm",
    )(*args)
    return outs


def _mm_kernel(*refs, rope_quarter, has_rope):
    if has_rope:
        a_ref, w_ref, cos_ref, sin_ref, o_ref, wb_ref = refs
    else:
        a_ref, w_ref, o_ref, wb_ref = refs

    @pl.when(pl.program_id(1) == 0)
    def _():
        wb_ref[...] = w_ref[...].astype(jnp.bfloat16)

    acc = jnp.dot(a_ref[...], wb_ref[...], preferred_element_type=jnp.float32)
    if has_rope:
        q = rope_quarter
        cs = cos_ref[...]
        sn = sin_ref[...]
        lane = lax.broadcasted_iota(jnp.int32, (acc.shape[0], LANES), 1)
        first = (lane % (2 * q)) < q
        for c in range(acc.shape[1] // LANES):
            a = acc[:, c * LANES:(c + 1) * LANES]
            partner = jnp.where(first, pltpu.roll(a, LANES - q, 1), pltpu.roll(a, q, 1))
            o_ref[:, c * LANES:(c + 1) * LANES] = (a * cs + partner * sn).astype(o_ref.dtype)
    else:
        o_ref[...] = acc.astype(o_ref.dtype)


def _matmul(a, w, layer, *, row_start, n_rows, out_dtype, rope=None, tm=1024, tn=512):
    k = a.shape[1]
    n = w.shape[2]
    tn = min(tn, n)
    m_off = row_start // tm
    in_specs = [
        pl.BlockSpec((tm, k), lambda j, i: (i + m_off, 0)),
        pl.BlockSpec((None, k, tn), lambda j, i: (layer, 0, j)),
    ]
    args = [a, w]
    has_rope = rope is not None
    quarter = 0
    if has_rope:
        cos_t, sin_t, quarter = rope
        per = DEC_SEQ // tm
        in_specs += [pl.BlockSpec((tm, LANES), lambda j, i: (i % per, 0))] * 2
        args += [cos_t, sin_t]
    return pl.pallas_call(
        functools.partial(_mm_kernel, rope_quarter=quarter, has_rope=has_rope),
        out_shape=jax.ShapeDtypeStruct((n_rows, n), out_dtype),
        grid=(n // tn, n_rows // tm),
        in_specs=in_specs,
        out_specs=pl.BlockSpec((tm, tn), lambda j, i: (i, j)),
        scratch_shapes=[pltpu.VMEM((k, tn), jnp.bfloat16)],
        compiler_params=_cparams(("arbitrary", "arbitrary")),
        name="proj",
    )(*args)


def _rope_tables(head_dim):
    quarter = head_dim // 4
    pos = jnp.arange(DEC_SEQ)
    row = (pos // GRID_W).astype(jnp.float32)
    col = (pos % GRID_W).astype(jnp.float32)
    inv = ROPE_BASE ** (-jnp.arange(quarter, dtype=jnp.float32) / quarter)
    ang_r = row[:, None] * inv
    ang_c = col[:, None] * inv
    ang = jnp.concatenate([ang_r, ang_r, ang_c, ang_c], axis=1)
    sign = jnp.concatenate([-jnp.ones(quarter), jnp.ones(quarter)] * 2)
    reps = LANES // head_dim
    cos_t = jnp.tile(jnp.cos(ang), (1, reps))
    sin_t = jnp.tile(jnp.sin(ang) * sign[None, :], (1, reps))
    return cos_t.astype(jnp.float32), sin_t.astype(jnp.float32), quarter


def _diff_attn_kernel(*refs, n_parts, lam_init):
    q_ref = refs[0]
    k_refs = refs[1:1 + n_parts]
    v_refs = refs[1 + n_parts:1 + 2 * n_parts]
    lam_ref, g_ref, o_ref = refs[1 + 2 * n_parts:]
    scale = DK_A ** -0.5
    lp = lam_ref[...]
    lam = (jnp.exp(jnp.sum(lp[0:1] * lp[1:2], axis=-1, keepdims=True))
           - jnp.exp(jnp.sum(lp[2:3] * lp[3:4], axis=-1, keepdims=True)) + lam_init)
    q = q_ref[...]
    probs = []
    for c in range(2):
        qc = q[:, c * DK_A:(c + 1) * DK_A]
        s_parts = []
        for kr in k_refs:
            kc = kr[:, c * DK_A:(c + 1) * DK_A].astype(jnp.bfloat16)
            s_parts.append(lax.dot_general(qc, kc, (((1,), (1,)), ((), ())),
                                           preferred_element_type=jnp.float32) * scale)
        m = s_parts[0].max(axis=-1, keepdims=True)
        for s in s_parts[1:]:
            m = jnp.maximum(m, s.max(axis=-1, keepdims=True))
        e_parts = [jnp.exp(s - m) for s in s_parts]
        den = e_parts[0].sum(axis=-1, keepdims=True)
        for e in e_parts[1:]:
            den = den + e.sum(axis=-1, keepdims=True)
        inv = 1.0 / den
        probs.append([e * inv for e in e_parts])
    o = None
    for p in range(n_parts):
        a = (probs[0][p] - lam * probs[1][p]).astype(jnp.bfloat16)
        contrib = jnp.dot(a, v_refs[p][...].astype(jnp.bfloat16), preferred_element_type=jnp.float32)
        o = contrib if o is None else o + contrib
    ms = jnp.mean(o * o, axis=-1, keepdims=True)
    o = o * lax.rsqrt(ms + LN_EPS) * g_ref[...] * (1.0 - lam_init)
    o_ref[...] = o.astype(o_ref.dtype)


def _diff_attention(q, k_parts, v_parts, lam_p, subln_g, ia, lam_init, *, n_batch, n_q, tq,
                    out_row_start, prev_out):
    nqb = n_q // tq
    row_off = out_row_start // tq
    in_specs = [pl.BlockSpec((tq, DV_A), lambda b, h, i: (b * nqb + i, h))]
    args = [q]
    for arr, spec in k_parts + v_parts:
        in_specs.append(spec)
        args.append(arr)
    in_specs.append(pl.BlockSpec((None, 4, DK_A), lambda b, h, i: (ia, 0, 0)))
    in_specs.append(pl.BlockSpec((None, 1, DV_A), lambda b, h, i: (ia, 0, 0)))
    args += [lam_p, subln_g.reshape(-1, 1, DV_A)]
    aliases = {}
    if prev_out is not None:
        in_specs.append(pl.BlockSpec(memory_space=pl.ANY))
        args.append(prev_out)
        aliases = {len(args) - 1: 0}

    def kern(*refs):
        if prev_out is not None:
            refs = refs[:-2] + refs[-1:]
        _diff_attn_kernel(*refs, n_parts=len(k_parts), lam_init=lam_init)

    return pl.pallas_call(
        kern,
        out_shape=jax.ShapeDtypeStruct((T_ALL, D_MODEL), jnp.bfloat16),
        grid=(n_batch, H_A, nqb),
        in_specs=in_specs,
        out_specs=pl.BlockSpec((tq, DV_A), lambda b, h, i: (row_off + b * nqb + i, h)),
        input_output_aliases=aliases,
        compiler_params=_cparams(("arbitrary", "arbitrary", "arbitrary")),
        name="diff_attn",
    )(*args)


def _gqa_kernel(*refs, n_parts, band, tq, n_lat):
    q_ref = refs[0]
    k_refs = refs[1:1 + n_parts]
    v_refs = refs[1 + n_parts:1 + 2 * n_parts]
    sink_ref, o_ref = refs[1 + 2 * n_parts:]
    scale = DH_B ** -0.5
    i = pl.program_id(1)
    q = q_ref[...]
    sink = sink_ref[...]
    grp = lax.broadcasted_iota(jnp.int32, (G_B * tq, 1), 0) // tq
    if band:
        start = pl.multiple_of(jnp.clip(i * tq - WINDOW, 0, n_lat - band), WINDOW)
        qpos = i * tq + (lax.broadcasted_iota(jnp.int32, (G_B * tq, band), 0) & (tq - 1))
        kpos = start + lax.broadcasted_iota(jnp.int32, (G_B * tq, band), 1)
        valid = jnp.abs(qpos - kpos) <= WINDOW
    k_vals, v_vals = [], []
    for p in range(n_parts):
        if band and p == n_parts - 1:
            k_vals.append(k_refs[p][pl.ds(start, band), :].astype(jnp.bfloat16))
            v_vals.append(v_refs[p][pl.ds(start, band), :].astype(jnp.bfloat16))
        else:
            k_vals.append(k_refs[p][...].astype(jnp.bfloat16))
            v_vals.append(v_refs[p][...].astype(jnp.bfloat16))
    for kk in range(KV_B):
        heads = [kk * G_B + g for g in range(G_B)]
        qs = jnp.concatenate([q[:, h * DH_B:(h + 1) * DH_B] for h in heads], axis=0)
        sk = jnp.zeros((G_B * tq, 1), jnp.float32)
        for g, h in enumerate(heads):
            sk = jnp.where(grp == g, sink[:, h:h + 1], sk)
        s_parts = []
        for p in range(n_parts):
            kc = k_vals[p][:, kk * DH_B:(kk + 1) * DH_B]
            s = lax.dot_general(qs, kc, (((1,), (1,)), ((), ())),
                                preferred_element_type=jnp.float32) * scale
            if band and p == n_parts - 1:
                s = jnp.where(valid, s, -1e30)
            s_parts.append(s)
        m = sk
        for s in s_parts:
            m = jnp.maximum(m, s.max(axis=-1, keepdims=True))
        e_parts = [jnp.exp(s - m) for s in s_parts]
        den = jnp.exp(sk - m)
        for e in e_parts:
            den = den + e.sum(axis=-1, keepdims=True)
        inv = 1.0 / den
        o = None
        for p in range(n_parts):
            pr = (e_parts[p] * inv).astype(jnp.bfloat16)
            contrib = jnp.dot(pr, v_vals[p][:, kk * DH_B:(kk + 1) * DH_B],
                              preferred_element_type=jnp.float32)
            o = contrib if o is None else o + contrib
        o_ref[:, kk * G_B * DH_B:(kk + 1) * G_B * DH_B] = jnp.concatenate(
            [o[g * tq:(g + 1) * tq] for g in range(G_B)], axis=1).astype(o_ref.dtype)


def _gqa_attention(q, k_parts, v_parts, sink, ib, *, n_batch, n_q, tq, band, out_row_start, prev_out):
    nqb = n_q // tq
    row_off = out_row_start // tq
    in_specs = [pl.BlockSpec((tq, D_MODEL), lambda b, i: (b * nqb + i, 0))]
    args = [q]
    for arr, spec in k_parts + v_parts:
        in_specs.append(spec)
        args.append(arr)
    in_specs.append(pl.BlockSpec((None, 1, H_B), lambda b, i: (ib, 0, 0)))
    args.append(sink.reshape(-1, 1, H_B))
    aliases = {}
    if prev_out is not None:
        in_specs.append(pl.BlockSpec(memory_space=pl.ANY))
        args.append(prev_out)
        aliases = {len(args) - 1: 0}

    def kern(*refs):
        if prev_out is not None:
            refs = refs[:-2] + refs[-1:]
        _gqa_kernel(*refs, n_parts=len(k_parts), band=band, tq=tq, n_lat=n_q)

    return pl.pallas_call(
        kern,
        out_shape=jax.ShapeDtypeStruct((T_ALL, D_MODEL), jnp.bfloat16),
        grid=(n_batch, nqb),
        in_specs=in_specs,
        out_specs=pl.BlockSpec((tq, D_MODEL), lambda b, i: (row_off + b * nqb + i, 0)),
        input_output_aliases=aliases,
        compiler_params=_cparams(("arbitrary", "arbitrary")),
        name="gqa_attn",
    )(*args)


def _pool_kernel(h_ref, w_ref, b_ref, ls_ref, o_ref, *, rows):
    i = pl.program_id(0)
    g = pl.program_id(1)
    seq = jnp.where(i * rows < T_P, SEQ, DEC_SEQ)
    pos = lax.broadcasted_iota(jnp.int32, (rows, 1), 0) & (seq - 1)

    def shifted(a, k):
        ok = (pos + k >= 0) & (pos + k < seq)
        return jnp.where(ok, pltpu.roll(a, (-k) % rows, 0), 0.0)

    for gi, wsz in enumerate(POOL_SIZES):
        @pl.when(g == gi)
        def _(wsz=wsz):
            h = h_ref[...].astype(jnp.float32)
            half = wsz // 2
            fwd = h
            m = 1
            while m < half:
                fwd = fwd + shifted(fwd, m)
                m *= 2
            bwd = shifted(h, -1)
            m = 1
            while m < half:
                bwd = bwd + shifted(bwd, -m)
                m *= 2
            lo = jnp.maximum(pos - half, 0)
            hi = jnp.minimum(pos + half - 1, seq - 1)
            cnt = (hi - lo + 1).astype(jnp.float32)
            diff = ((fwd + bwd) / cnt - h).astype(jnp.bfloat16)
            y = jnp.dot(diff, w_ref[...].astype(jnp.bfloat16), preferred_element_type=jnp.float32)
            o_ref[...] = (y + b_ref[...]) * ls_ref[...]


def _pool_mixer(h, pool_w, pool_b, pool_scale, ic):
    rows = DEC_SEQ
    ng = len(POOL_SIZES)
    return pl.pallas_call(
        functools.partial(_pool_kernel, rows=rows),
        out_shape=jax.ShapeDtypeStruct((T_ALL, D_MODEL), jnp.float32),
        grid=(T_ALL // rows, ng),
        in_specs=[
            pl.BlockSpec((rows, POOL_GROUP), lambda i, g: (i, g)),
            pl.BlockSpec((None, None, POOL_GROUP, POOL_GROUP), lambda i, g: (ic, g, 0, 0)),
            pl.BlockSpec((None, None, 1, POOL_GROUP), lambda i, g: (ic, g, 0, 0)),
            pl.BlockSpec((None, None, 1, POOL_GROUP), lambda i, g: (ic, g, 0, 0)),
        ],
        out_specs=pl.BlockSpec((rows, POOL_GROUP), lambda i, g: (i, g)),
        compiler_params=_cparams(("arbitrary", "arbitrary")),
        name="pool",
    )(h, pool_w, pool_b.reshape(-1, ng, 1, POOL_GROUP), pool_scale.reshape(-1, ng, 1, POOL_GROUP))


def _router_kernel(h_ref, w_ref, b_ref, e_ref, g_ref, r_ref, cnt_ref, run_ref, *, tm):
    i = pl.program_id(0)

    @pl.when(i == 0)
    def _():
        run_ref[...] = jnp.zeros_like(run_ref)

    logits = jnp.dot(h_ref[...].astype(jnp.float32), w_ref[...],
                     precision=lax.Precision.HIGHEST,
                     preferred_element_type=jnp.float32) + b_ref[...]
    lane = lax.broadcasted_iota(jnp.int32, logits.shape, 1).astype(jnp.float32)
    work = logits
    vals, idxs = [], []
    member = jnp.zeros(logits.shape, jnp.float32)
    for _ in range(TOP_K):
        m = work.max(axis=-1, keepdims=True)
        idx = jnp.where(work == m, lane, float(N_EXPERTS)).min(axis=-1, keepdims=True)
        hit = lane == idx
        member = jnp.where(hit, 1.0, member)
        work = jnp.where(hit, -jnp.inf, work)
        vals.append(m)
        idxs.append(idx)
    es = [jnp.exp(v - vals[0]) for v in vals]
    den = es[0]
    for e in es[1:]:
        den = den + e
    r_i = lax.broadcasted_iota(jnp.int32, (tm, tm), 0)
    c_i = lax.broadcasted_iota(jnp.int32, (tm, tm), 1)
    tri = jnp.where(c_i < r_i, 1.0, 0.0).astype(jnp.bfloat16)
    before = jnp.dot(tri, member.astype(jnp.bfloat16), preferred_element_type=jnp.float32) + run_ref[...]
    for k in range(TOP_K):
        e_ref[:, k:k + 1] = idxs[k].astype(jnp.int32)
        g_ref[:, k:k + 1] = es[k] / den
        r_ref[:, k:k + 1] = jnp.where(lane == idxs[k], before, 0.0).sum(
            axis=-1, keepdims=True).astype(jnp.int32)
    run_ref[...] = run_ref[...] + member.sum(axis=0, keepdims=True)
    cnt_ref[...] = run_ref[...].astype(jnp.int32)


def _router(h, w_router, b_router, layer):
    tm = 256
    small = lambda dt: jax.ShapeDtypeStruct((T_ALL, TOP_K), dt)
    return pl.pallas_call(
        functools.partial(_router_kernel, tm=tm),
        out_shape=[small(jnp.int32), small(jnp.float32), small(jnp.int32),
                   jax.ShapeDtypeStruct((1, N_EXPERTS), jnp.int32)],
        grid=(T_ALL // tm,),
        in_specs=[
            pl.BlockSpec((tm, D_MODEL), lambda i: (i, 0)),
            pl.BlockSpec((None, D_MODEL, N_EXPERTS), lambda i: (layer, 0, 0)),
            pl.BlockSpec((None, 1, N_EXPERTS), lambda i: (layer, 0, 0)),
        ],
        out_specs=[pl.BlockSpec((tm, TOP_K), lambda i: (i, 0))] * 3
        + [pl.BlockSpec((1, N_EXPERTS), lambda i: (0, 0))],
        scratch_shapes=[pltpu.VMEM((1, N_EXPERTS), jnp.float32)],
        compiler_params=_cparams(("arbitrary",)),
        name="router",
    )(h, w_router, b_router.reshape(DEPTH, 1, N_EXPERTS))


def _row_copy_kernel(src_idx, dst_idx, src_ref, *rest, n):
    dst_ref, sem = rest[-2:]

    def wait_batch():
        pltpu.make_async_copy(src_ref.at[pl.ds(0, DMA_BATCH)], dst_ref.at[pl.ds(0, DMA_BATCH)],
                              sem).wait()

    def batch(bi, carry):
        for u in range(DMA_BATCH):
            i = bi * DMA_BATCH + u
            pltpu.make_async_copy(src_ref.at[pl.ds(src_idx[i], 1)],
                                  dst_ref.at[pl.ds(dst_idx[i], 1)], sem).start()

        @pl.when(bi > 0)
        def _():
            wait_batch()
        return carry

    lax.fori_loop(0, n // DMA_BATCH, batch, 0)
    wait_batch()


def _row_copy(src, src_idx, dst_idx, n_dst_rows, dst_init=None):
    n = src_idx.shape[0]
    any_spec = pl.BlockSpec(memory_space=pl.ANY)
    args = [src_idx, dst_idx, src]
    aliases = {}
    if dst_init is not None:
        args.append(dst_init)
        aliases = {3: 0}
    return pl.pallas_call(
        functools.partial(_row_copy_kernel, n=n),
        out_shape=jax.ShapeDtypeStruct((n_dst_rows, src.shape[1]), src.dtype),
        grid_spec=pltpu.PrefetchScalarGridSpec(
            num_scalar_prefetch=2,
            grid=(1,),
            in_specs=[any_spec] * (len(args) - 2),
            out_specs=any_spec,
            scratch_shapes=[pltpu.SemaphoreType.DMA],
        ),
        input_output_aliases=aliases,
        compiler_params=pltpu.CompilerParams(dimension_semantics=("arbitrary",),
                                             has_side_effects=True),
        name="row_copy",
    )(*args)


def _ffn_kernel(sb_e, sb_start, sb_nch, n_sb, xs_ref, wg_ref, wu_ref, wd_ref, bg_ref, bu_ref, bd_ref,
                y_ref, xbuf, yacc, wgb, wub, wdb, sem):
    s = pl.program_id(0)
    j = pl.program_id(1)
    nf = pl.num_programs(1)
    nch = sb_nch[s]
    start = sb_start[s]

    def x_copy(c):
        r = pl.multiple_of(start + c * MOE_CHUNK, MOE_CHUNK)
        return pltpu.make_async_copy(xs_ref.at[pl.ds(r, MOE_CHUNK)],
                                     xbuf.at[pl.ds(c * MOE_CHUNK, MOE_CHUNK)], sem.at[0])

    def y_copy(c):
        r = pl.multiple_of(start + c * MOE_CHUNK, MOE_CHUNK)
        return pltpu.make_async_copy(yacc.at[pl.ds(c * MOE_CHUNK, MOE_CHUNK)],
                                     y_ref.at[pl.ds(r, MOE_CHUNK)], sem.at[1])

    def for_chunks(fn):
        def body(c, carry):
            fn(c)
            return carry
        lax.fori_loop(0, nch, body, 0)

    @pl.when((j == 0) & (nch > 0))
    def _():
        for_chunks(lambda c: x_copy(c).start())
        yacc[...] = jnp.broadcast_to(bd_ref[...], yacc.shape)
        for_chunks(lambda c: x_copy(c).wait())

    @pl.when(nch > 0)
    def _():
        wgb[...] = wg_ref[...].astype(jnp.bfloat16)
        wub[...] = wu_ref[...].astype(jnp.bfloat16)
        wdb[...] = wd_ref[...].astype(jnp.bfloat16)

        def chunk(c):
            rows = pl.ds(pl.multiple_of(c * MOE_CHUNK, MOE_CHUNK), MOE_CHUNK)
            x = xbuf[rows, :].astype(jnp.bfloat16)
            gate = jnp.dot(x, wgb[...], preferred_element_type=jnp.float32) + bg_ref[...]
            up = jnp.dot(x, wub[...], preferred_element_type=jnp.float32) + bu_ref[...]
            gate = jnp.minimum(gate, SWIGLU_LIMIT)
            up = jnp.clip(up, -SWIGLU_LIMIT, SWIGLU_LIMIT)
            act = (up + 1.0) * (gate / (1.0 + jnp.exp(-SWIGLU_ALPHA * gate)))
            yacc[rows, :] += jnp.dot(act.astype(jnp.bfloat16), wdb[...],
                                     preferred_element_type=jnp.float32)
        for_chunks(chunk)

    @pl.when((j == nf - 1) & (nch > 0))
    def _():
        for_chunks(lambda c: y_copy(c).start())
        for_chunks(lambda c: y_copy(c).wait())


def _expert_ffn(xs, tables, w_gu, b_gu, w_down, b_down, layer):
    sb_e, sb_start, sb_nch, n_sb = tables
    nf = D_FF // MOE_TF

    def live_j(s, j, nsb):
        return jnp.where(s < nsb[0], j, nf - 1)

    grid_spec = pltpu.PrefetchScalarGridSpec(
        num_scalar_prefetch=4,
        grid=(N_SUPER, nf),
        in_specs=[
            pl.BlockSpec(memory_space=pl.ANY),
            pl.BlockSpec((None, None, D_MODEL, MOE_TF),
                         lambda s, j, e, st, nc, nsb: (layer, e[s], 0, live_j(s, j, nsb))),
            pl.BlockSpec((None, None, D_MODEL, MOE_TF),
                         lambda s, j, e, st, nc, nsb: (layer, e[s], 0, nf + live_j(s, j, nsb))),
            pl.BlockSpec((None, None, MOE_TF, D_MODEL),
                         lambda s, j, e, st, nc, nsb: (layer, e[s], live_j(s, j, nsb), 0)),
            pl.BlockSpec((None, None, 1, MOE_TF),
                         lambda s, j, e, st, nc, nsb: (layer, e[s], 0, live_j(s, j, nsb))),
            pl.BlockSpec((None, None, 1, MOE_TF),
                         lambda s, j, e, st, nc, nsb: (layer, e[s], 0, nf + live_j(s, j, nsb))),
            pl.BlockSpec((None, None, 1, D_MODEL),
                         lambda s, j, e, st, nc, nsb: (layer, e[s], 0, 0)),
        ],
        out_specs=pl.BlockSpec(memory_space=pl.ANY),
        scratch_shapes=[
            pltpu.VMEM((MOE_TMAX, D_MODEL), jnp.float32),
            pltpu.VMEM((MOE_TMAX, D_MODEL), jnp.float32),
            pltpu.VMEM((D_MODEL, MOE_TF), jnp.bfloat16),
            pltpu.VMEM((D_MODEL, MOE_TF), jnp.bfloat16),
            pltpu.VMEM((MOE_TF, D_MODEL), jnp.bfloat16),
            pltpu.SemaphoreType.DMA((2,)),
        ],
    )
    return pl.pallas_call(
        _ffn_kernel,
        out_shape=jax.ShapeDtypeStruct((N_SLOTS, D_MODEL), jnp.float32),
        grid_spec=grid_spec,
        compiler_params=pltpu.CompilerParams(dimension_semantics=("arbitrary", "arbitrary"),
                                             vmem_limit_bytes=VMEM_LIMIT, has_side_effects=True),
        name="expert_ffn",
    )(sb_e, sb_start, sb_nch, n_sb, xs, w_gu, w_gu, w_down,
      b_gu.reshape(DEPTH, N_EXPERTS, 1, 2 * D_FF), b_gu.reshape(DEPTH, N_EXPERTS, 1, 2 * D_FF),
      b_down.reshape(DEPTH, N_EXPERTS, 1, D_MODEL))


def _superblock_tables(counts):
    padded = (counts + MOE_CHUNK - 1) // MOE_CHUNK * MOE_CHUNK
    pend = jnp.cumsum(padded)
    pstart = pend - padded
    nsb = (padded + MOE_TMAX - 1) // MOE_TMAX
    sb_end = jnp.cumsum(nsb)
    sb_first = sb_end - nsb
    n_sb = sb_end[-1]
    s = jnp.arange(N_SUPER, dtype=jnp.int32)
    e_of = jnp.minimum(jnp.searchsorted(sb_end, s, side='right'), N_EXPERTS - 1).astype(jnp.int32)
    live = s < n_sb
    last_e = jnp.max(jnp.where(nsb > 0, jnp.arange(N_EXPERTS), 0)).astype(jnp.int32)
    local = s - sb_first[e_of]
    start = pstart[e_of] + local * MOE_TMAX
    rows = jnp.clip(padded[e_of] - local * MOE_TMAX, 0, MOE_TMAX)
    sb_e = jnp.where(live, e_of, last_e).astype(jnp.int32)
    sb_start = jnp.where(live, start, 0).astype(jnp.int32)
    sb_nch = jnp.where(live, rows // MOE_CHUNK, 0).astype(jnp.int32)
    return pstart, (sb_e, sb_start, sb_nch, n_sb.reshape(1).astype(jnp.int32))


def _moe(h, w_router, b_router, w_gu, b_gu, w_down, b_down, layer):
    top_e, gates, rank, counts = _router(h, w_router, b_router, layer)
    pstart, tables = _superblock_tables(counts[0])
    dest = (pstart[top_e] + rank).astype(jnp.int32)
    tok = jnp.broadcast_to(jnp.arange(T_ALL, dtype=jnp.int32)[:, None], (T_ALL, TOP_K))
    xs = _row_copy(h, tok.reshape(-1), dest.reshape(-1), N_SLOTS,
                   dst_init=jnp.zeros((N_SLOTS, D_MODEL), jnp.float32))
    ys = _expert_ffn(xs, tables, w_gu, b_gu, w_down, b_down, layer)
    dst_rows = (jnp.arange(TOP_K, dtype=jnp.int32)[None, :] * T_ALL + tok).reshape(-1)
    yg = _row_copy(ys, dest.reshape(-1), dst_rows, TOP_K * T_ALL)
    return yg.reshape(TOP_K, T_ALL, D_MODEL), gates


def kernel(x_prompt, x_sample, cache_diff_k, cache_diff_v, cache_win_k, cache_win_v, c, c_ctx, w_ada, b_ada, ln_g, ln_b, diff_wq, diff_wk, diff_wv, diff_wo, diff_lambda, diff_subln_g, win_wq, win_wk, win_wv, win_wo, win_sink, pool_w, pool_b, pool_scale, moe_w_router, moe_b_router, moe_w_gu, moe_b_gu, moe_w_down, moe_b_down):
    f32, bf16 = jnp.float32, jnp.bfloat16
    x = jnp.concatenate([x_prompt.reshape(T_P, D_MODEL), x_sample.reshape(T_S, D_MODEL)], axis=0)
    cvec = jnp.concatenate([c_ctx[None, :], c, jnp.zeros((COND_ROWS - N_COND, D_MODEL), f32)], axis=0)
    ada4 = _ada_all(cvec, w_ada, b_ada).reshape(DEPTH, COND_ROWS, N_ADA, D_MODEL)
    rope_a = _rope_tables(DK_A)
    rope_b = _rope_tables(DH_B)
    n_la = cache_diff_k.shape[1]
    n_lb = cache_win_k.shape[1]
    cdk = cache_diff_k.reshape(DEC_BATCH, n_la, PAST_LEN, D_MODEL)
    cdv = cache_diff_v.reshape(DEC_BATCH, n_la, PAST_LEN, D_MODEL)
    cwk = cache_win_k.reshape(DEC_BATCH, n_lb, PAST_LEN, KV_B * DH_B)
    cwv = cache_win_v.reshape(DEC_BATCH, n_lb, PAST_LEN, KV_B * DH_B)

    (h,) = _modnorm(x, [], None, ada4, 0, 0, ln_g, ln_b, (0, 0), do_norm=False, emit_h=True,
                    shift_row=0, scale_row=1)
    st_dk, st_dv, st_wk, st_wv = [], [], [], []
    ia = ib = ic = 0
    for i in range(DEPTH):
        kind = i % N_MIXERS
        if kind == 0:
            lam_init = 0.8 - 0.6 * math.exp(-0.3 * i)
            pj = functools.partial(_matmul, h, layer=ia)
            q_p = pj(diff_wq, row_start=0, n_rows=T_P, out_dtype=bf16)
            k_p = pj(diff_wk, row_start=0, n_rows=T_P, out_dtype=f32)
            v_p = pj(diff_wv, row_start=0, n_rows=T_P, out_dtype=f32)
            q_s = pj(diff_wq, row_start=T_P, n_rows=T_S, out_dtype=bf16, rope=rope_a)
            k_s = pj(diff_wk, row_start=T_P, n_rows=T_S, out_dtype=bf16, rope=rope_a)
            v_s = pj(diff_wv, row_start=T_P, n_rows=T_S, out_dtype=bf16)
            seq_spec = pl.BlockSpec((SEQ, DV_A), lambda b, hh, qi: (b, hh))
            o = _diff_attention(q_p, [(k_p, seq_spec)], [(v_p, seq_spec)], diff_lambda, diff_subln_g,
                                ia, lam_init, n_batch=BATCH, n_q=SEQ, tq=SEQ, out_row_start=0,
                                prev_out=None)
            cache_spec = pl.BlockSpec((None, None, PAST_LEN, DV_A), lambda b, hh, qi, ia=ia: (b, ia, 0, hh))
            lat_spec = pl.BlockSpec((DEC_SEQ, DV_A), lambda b, hh, qi: (b, hh))
            o = _diff_attention(q_s, [(cdk, cache_spec), (k_s, lat_spec)],
                                [(cdv, cache_spec), (v_s, lat_spec)], diff_lambda, diff_subln_g,
                                ia, lam_init, n_batch=DEC_BATCH, n_q=DEC_SEQ, tq=256,
                                out_row_start=T_P, prev_out=o)
            sub = _matmul(o, diff_wo, ia, row_start=0, n_rows=T_ALL, out_dtype=f32)
            st_dk.append(k_p.reshape(BATCH, SEQ, H_A, 2, DK_A))
            st_dv.append(v_p.reshape(BATCH, SEQ, H_A, DV_A))
            ia += 1
        elif kind == 1:
            pj = functools.partial(_matmul, h, layer=ib)
            q_p = pj(win_wq, row_start=0, n_rows=T_P, out_dtype=bf16)
            k_p = pj(win_wk, row_start=0, n_rows=T_P, out_dtype=f32)
            v_p = pj(win_wv, row_start=0, n_rows=T_P, out_dtype=f32)
            q_s = pj(win_wq, row_start=T_P, n_rows=T_S, out_dtype=bf16, rope=rope_b)
            k_s = pj(win_wk, row_start=T_P, n_rows=T_S, out_dtype=bf16, rope=rope_b)
            v_s = pj(win_wv, row_start=T_P, n_rows=T_S, out_dtype=bf16)
            kvw = KV_B * DH_B
            seq_spec = pl.BlockSpec((SEQ, kvw), lambda b, qi: (b, 0))
            o = _gqa_attention(q_p, [(k_p, seq_spec)], [(v_p, seq_spec)], win_sink, ib,
                               n_batch=BATCH, n_q=SEQ, tq=SEQ, band=0, out_row_start=0, prev_out=None)
            cache_spec = pl.BlockSpec((None, None, PAST_LEN, kvw), lambda b, qi, ib=ib: (b, ib, 0, 0))
            lat_spec = pl.BlockSpec((DEC_SEQ, kvw), lambda b, qi: (b, 0))
            o = _gqa_attention(q_s, [(cwk, cache_spec), (k_s, lat_spec)],
                               [(cwv, cache_spec), (v_s, lat_spec)], win_sink, ib,
                               n_batch=DEC_BATCH, n_q=DEC_SEQ, tq=WINDOW, band=3 * WINDOW,
                               out_row_start=T_P, prev_out=o)
            sub = _matmul(o, win_wo, ib, row_start=0, n_rows=T_ALL, out_dtype=f32)
            st_wk.append(k_p.reshape(BATCH, SEQ, KV_B, DH_B))
            st_wv.append(v_p.reshape(BATCH, SEQ, KV_B, DH_B))
            ib += 1
        else:
            sub = _pool_mixer(h, pool_w, pool_b, pool_scale, ic)
            ic += 1
        x, h = _modnorm(x, [sub], None, ada4, i, i, ln_g, ln_b, (i, 0), do_norm=True, emit_h=True,
                        h_dtype=f32, gate_row=2, shift_row=3, scale_row=4)
        yg, gates = _moe(h, moe_w_router, moe_b_router, moe_w_gu, moe_b_gu,
                         moe_w_down, moe_b_down, i)
        last = i == DEPTH - 1
        outs = _modnorm(x, [yg], gates, ada4, i, min(i + 1, DEPTH - 1), ln_g, ln_b, (i, 1),
                        do_norm=True, emit_h=not last, gate_row=5, shift_row=0, scale_row=1)
        x = outs[0]
        if not last:
            h = outs[1]
    y_prompt = x[:T_P].reshape(BATCH, SEQ, D_MODEL)
    y_sample = x[T_P:].reshape(DEC_BATCH, DEC_SEQ, D_MODEL)
    return (y_prompt, y_sample, jnp.stack(st_dk, axis=1), jnp.stack(st_dv, axis=1),
            jnp.stack(st_wk, axis=1), jnp.stack(st_wv, axis=1))
```

```python
import functools
import math

import jax
import jax.numpy as jnp
from jax import lax
from jax.experimental import pallas as pl
from jax.experimental.pallas import tpu as pltpu

D_MODEL = 2048
BATCH = 16
SEQ = 256
DEPTH = 4
DEC_BATCH = 2
DEC_SEQ = 2048
PAST_LEN = 512
GRID_W = 64
N_MIXERS = 3
H_A = 8
DK_A = D_MODEL // (2 * H_A)
DV_A = 2 * DK_A
DH_B = 64
H_B = D_MODEL // DH_B
KV_B = 4
G_B = H_B // KV_B
WINDOW = 128
POOL_SIZES = (2, 4, 8, 16)
POOL_GROUP = D_MODEL // len(POOL_SIZES)
N_EXPERTS = 32
TOP_K = 4
D_FF = D_MODEL
SWIGLU_ALPHA = 1.702
SWIGLU_LIMIT = 7.0
ROPE_BASE = 10000.0
LN_EPS = 1e-5
DEEPNORM_ALPHA = (2 * DEPTH) ** 0.25
N_ADA = 6

T_P = BATCH * SEQ
T_S = DEC_BATCH * DEC_SEQ
T_ALL = T_P + T_S
N_COND = 1 + DEC_BATCH
COND_ROWS = 8

LANES = 128
VMEM_LIMIT = 52 * 1024 * 1024
BF16_ROWS = 16
MOE_TT = 512
MOE_NT = T_ALL // MOE_TT
MOE_ALIGN = BF16_ROWS
MOE_SMAX = MOE_TT * TOP_K + N_EXPERTS * MOE_ALIGN
N_SLOTS = MOE_NT * MOE_SMAX
MOE_CHUNK = 256
MOE_TMAX = 2048
MOE_TF = 256
N_SUPER = N_EXPERTS + N_SLOTS // (MOE_TMAX - MOE_TT)
DMA_SIZES = (512, 256, 128, 64, 32, 16)


def _cparams(sem, vmem=VMEM_LIMIT):
    return pltpu.CompilerParams(dimension_semantics=sem, vmem_limit_bytes=vmem)


def _cond_group(row):
    return jnp.maximum(row // DEC_SEQ - (T_P // DEC_SEQ - 1), 0)


def _ada_kernel(c_ref, w_ref, b_ref, o_ref):
    cv = c_ref[...]
    s = (cv / (1.0 + jnp.exp(-cv))).astype(jnp.bfloat16)
    o_ref[...] = jnp.dot(s, w_ref[...].astype(jnp.bfloat16),
                         preferred_element_type=jnp.float32) + b_ref[...]


def _ada_all(cvec, w_ada, b_ada):
    tn = 1024
    n = N_ADA * D_MODEL
    return pl.pallas_call(
        _ada_kernel,
        out_shape=jax.ShapeDtypeStruct((DEPTH, COND_ROWS, n), jnp.float32),
        grid=(DEPTH, n // tn),
        in_specs=[
            pl.BlockSpec((COND_ROWS, D_MODEL), lambda l, j: (0, 0)),
            pl.BlockSpec((None, D_MODEL, tn), lambda l, j: (l, 0, j)),
            pl.BlockSpec((None, 1, tn), lambda l, j: (l, 0, j)),
        ],
        out_specs=pl.BlockSpec((None, COND_ROWS, tn), lambda l, j: (l, 0, j)),
        compiler_params=_cparams(("arbitrary", "arbitrary")),
        name="ada",
    )(cvec, w_ada, b_ada.reshape(DEPTH, 1, n))


def _post_norm(x, sub, gate, lng_ref, lnb_ref):
    xf = DEEPNORM_ALPHA * x + gate * sub
    mu = jnp.mean(xf, axis=-1, keepdims=True)
    xc = xf - mu
    var = jnp.mean(xc * xc, axis=-1, keepdims=True)
    return xc * lax.rsqrt(var + LN_EPS) * lng_ref[...] + lnb_ref[...]


def _modulate(x, mod_ref, shift_row, scale_row):
    return x * (1.0 + mod_ref[scale_row:scale_row + 1, :]) + mod_ref[shift_row:shift_row + 1, :]


def _modnorm_kernel(*refs, do_norm, gate_row, shift_row, scale_row):
    if do_norm:
        x_ref, sub_ref, mod_ref, lng_ref, lnb_ref, xo_ref, h_ref = refs
        x = _post_norm(x_ref[...], sub_ref[...], mod_ref[gate_row:gate_row + 1, :], lng_ref, lnb_ref)
        xo_ref[...] = x
    else:
        x_ref, mod_ref, h_ref = refs
        x = x_ref[...]
    h_ref[...] = _modulate(x, mod_ref, shift_row, scale_row).astype(h_ref.dtype)


def _ada_spec(layer, tm, rows_per_step=1):
    return pl.BlockSpec((None, None, N_ADA, D_MODEL),
                        lambda i, *_: (layer, _cond_group(i * tm * rows_per_step), 0, 0))


def _ln_args(ln_g, ln_b, layer, which):
    spec = pl.BlockSpec((None, None, 1, D_MODEL), lambda i, *_: (layer, which, 0, 0))
    return [spec, spec], [ln_g.reshape(DEPTH, 2, 1, D_MODEL), ln_b.reshape(DEPTH, 2, 1, D_MODEL)]


def _modnorm(x, sub, ada4, layer, ln_g, ln_b, *, h_dtype, gate_row=0, shift_row=0, scale_row=0):
    tm = 256
    do_norm = sub is not None
    row_spec = pl.BlockSpec((tm, D_MODEL), lambda i: (i, 0))
    in_specs, args = [row_spec], [x]
    if do_norm:
        in_specs.append(row_spec)
        args.append(sub)
    in_specs.append(_ada_spec(layer, tm))
    args.append(ada4)
    out_shape = [jax.ShapeDtypeStruct((T_ALL, D_MODEL), h_dtype)]
    out_specs = [row_spec]
    if do_norm:
        ln_specs, ln_arrs = _ln_args(ln_g, ln_b, layer, 0)
        in_specs += ln_specs
        args += ln_arrs
        out_shape.insert(0, jax.ShapeDtypeStruct((T_ALL, D_MODEL), jnp.float32))
        out_specs.insert(0, row_spec)
    kern = functools.partial(_modnorm_kernel, do_norm=do_norm, gate_row=gate_row,
                             shift_row=shift_row, scale_row=scale_row)
    return pl.pallas_call(
        kern, out_shape=out_shape, grid=(T_ALL // tm,), in_specs=in_specs, out_specs=out_specs,
        compiler_params=_cparams(("arbitrary",)), name="modnorm",
    )(*args)


def _mm_kernel(*refs, rope_quarter, has_rope):
    if has_rope:
        a_ref, w_ref, cos_ref, sin_ref, o_ref, wb_ref = refs
    else:
        a_ref, w_ref, o_ref, wb_ref = refs

    @pl.when(pl.program_id(1) == 0)
    def _():
        wb_ref[...] = w_ref[...].astype(jnp.bfloat16)

    acc = jnp.dot(a_ref[...], wb_ref[...], preferred_element_type=jnp.float32)
    if has_rope:
        q = rope_quarter
        cs = cos_ref[...]
        sn = sin_ref[...]
        lane = lax.broadcasted_iota(jnp.int32, (acc.shape[0], LANES), 1)
        first = (lane % (2 * q)) < q
        for c in range(acc.shape[1] // LANES):
            a = acc[:, c * LANES:(c + 1) * LANES]
            partner = jnp.where(first, pltpu.roll(a, LANES - q, 1), pltpu.roll(a, q, 1))
            o_ref[:, c * LANES:(c + 1) * LANES] = (a * cs + partner * sn).astype(o_ref.dtype)
    else:
        o_ref[...] = acc.astype(o_ref.dtype)


def _matmul(a, w, layer, *, row_start, n_rows, out_dtype, rope=None, tm=1024, tn=512):
    k = a.shape[1]
    n = w.shape[2]
    tn = min(tn, n)
    m_off = row_start // tm
    in_specs = [
        pl.BlockSpec((tm, k), lambda j, i: (i + m_off, 0)),
        pl.BlockSpec((None, k, tn), lambda j, i: (layer, 0, j)),
    ]
    args = [a, w]
    has_rope = rope is not None
    quarter = 0
    if has_rope:
        cos_t, sin_t, quarter = rope
        per = DEC_SEQ // tm
        in_specs += [pl.BlockSpec((tm, LANES), lambda j, i: (i % per, 0))] * 2
        args += [cos_t, sin_t]
    return pl.pallas_call(
        functools.partial(_mm_kernel, rope_quarter=quarter, has_rope=has_rope),
        out_shape=jax.ShapeDtypeStruct((n_rows, n), out_dtype),
        grid=(n // tn, n_rows // tm),
        in_specs=in_specs,
        out_specs=pl.BlockSpec((tm, tn), lambda j, i: (i, j)),
        scratch_shapes=[pltpu.VMEM((k, tn), jnp.bfloat16)],
        compiler_params=_cparams(("arbitrary", "arbitrary")),
        name="proj",
    )(*args)


def _rope_tables(head_dim):
    quarter = head_dim // 4
    pos = jnp.arange(DEC_SEQ)
    row = (pos // GRID_W).astype(jnp.float32)
    col = (pos % GRID_W).astype(jnp.float32)
    inv = ROPE_BASE ** (-jnp.arange(quarter, dtype=jnp.float32) / quarter)
    ang_r = row[:, None] * inv
    ang_c = col[:, None] * inv
    ang = jnp.concatenate([ang_r, ang_r, ang_c, ang_c], axis=1)
    sign = jnp.concatenate([-jnp.ones(quarter), jnp.ones(quarter)] * 2)
    reps = LANES // head_dim
    cos_t = jnp.tile(jnp.cos(ang), (1, reps))
    sin_t = jnp.tile(jnp.sin(ang) * sign[None, :], (1, reps))
    return cos_t.astype(jnp.float32), sin_t.astype(jnp.float32), quarter


def _diff_attn_kernel(*refs, n_parts, lam_init):
    q_ref = refs[0]
    k_refs = refs[1:1 + n_parts]
    v_refs = refs[1 + n_parts:1 + 2 * n_parts]
    lam_ref, g_ref, o_ref = refs[1 + 2 * n_parts:]
    scale = DK_A ** -0.5
    lp = lam_ref[...]
    lam = (jnp.exp(jnp.sum(lp[0:1] * lp[1:2], axis=-1, keepdims=True))
           - jnp.exp(jnp.sum(lp[2:3] * lp[3:4], axis=-1, keepdims=True)) + lam_init)
    q = q_ref[...]
    probs = []
    for c in range(2):
        qc = q[:, c * DK_A:(c + 1) * DK_A]
        s_parts = []
        for kr in k_refs:
            kc = kr[:, c * DK_A:(c + 1) * DK_A].astype(jnp.bfloat16)
            s_parts.append(lax.dot_general(qc, kc, (((1,), (1,)), ((), ())),
                                           preferred_element_type=jnp.float32) * scale)
        m = s_parts[0].max(axis=-1, keepdims=True)
        for s in s_parts[1:]:
            m = jnp.maximum(m, s.max(axis=-1, keepdims=True))
        e_parts = [jnp.exp(s - m) for s in s_parts]
        den = e_parts[0].sum(axis=-1, keepdims=True)
        for e in e_parts[1:]:
            den = den + e.sum(axis=-1, keepdims=True)
        inv = 1.0 / den
        probs.append([e * inv for e in e_parts])
    o = None
    for p in range(n_parts):
        a = (probs[0][p] - lam * probs[1][p]).astype(jnp.bfloat16)
        contrib = jnp.dot(a, v_refs[p][...].astype(jnp.bfloat16), preferred_element_type=jnp.float32)
        o = contrib if o is None else o + contrib
    ms = jnp.mean(o * o, axis=-1, keepdims=True)
    o = o * lax.rsqrt(ms + LN_EPS) * g_ref[...] * (1.0 - lam_init)
    o_ref[...] = o.astype(o_ref.dtype)


def _diff_attention(q, k_parts, v_parts, lam_p, subln_g, ia, lam_init, *, n_batch, n_q, tq,
                    out_row_start, prev_out):
    nqb = n_q // tq
    row_off = out_row_start // tq
    in_specs = [pl.BlockSpec((tq, DV_A), lambda b, h, i: (b * nqb + i, h))]
    args = [q]
    for arr, spec in k_parts + v_parts:
        in_specs.append(spec)
        args.append(arr)
    in_specs.append(pl.BlockSpec((None, 4, DK_A), lambda b, h, i: (ia, 0, 0)))
    in_specs.append(pl.BlockSpec((None, 1, DV_A), lambda b, h, i: (ia, 0, 0)))
    args += [lam_p, subln_g.reshape(-1, 1, DV_A)]
    aliases = {}
    if prev_out is not None:
        in_specs.append(pl.BlockSpec(memory_space=pl.ANY))
        args.append(prev_out)
        aliases = {len(args) - 1: 0}

    def kern(*refs):
        if prev_out is not None:
            refs = refs[:-2] + refs[-1:]
        _diff_attn_kernel(*refs, n_parts=len(k_parts), lam_init=lam_init)

    return pl.pallas_call(
        kern,
        out_shape=jax.ShapeDtypeStruct((T_ALL, D_MODEL), jnp.bfloat16),
        grid=(n_batch, H_A, nqb),
        in_specs=in_specs,
        out_specs=pl.BlockSpec((tq, DV_A), lambda b, h, i: (row_off + b * nqb + i, h)),
        input_output_aliases=aliases,
        compiler_params=_cparams(("arbitrary", "arbitrary", "arbitrary")),
        name="diff_attn",
    )(*args)


def _gqa_kernel(*refs, n_parts, band, tq, n_lat):
    q_ref = refs[0]
    k_refs = refs[1:1 + n_parts]
    v_refs = refs[1 + n_parts:1 + 2 * n_parts]
    sink_ref, o_ref = refs[1 + 2 * n_parts:]
    scale = DH_B ** -0.5
    i = pl.program_id(1)
    q = q_ref[...]
    sink = sink_ref[...]
    grp = lax.broadcasted_iota(jnp.int32, (G_B * tq, 1), 0) // tq
    if band:
        start = pl.multiple_of(jnp.clip(i * tq - WINDOW, 0, n_lat - band), WINDOW)
        qpos = i * tq + (lax.broadcasted_iota(jnp.int32, (G_B * tq, band), 0) & (tq - 1))
        kpos = start + lax.broadcasted_iota(jnp.int32, (G_B * tq, band), 1)
        valid = jnp.abs(qpos - kpos) <= WINDOW
    k_vals, v_vals = [], []
    for p in range(n_parts):
        if band and p == n_parts - 1:
            k_vals.append(k_refs[p][pl.ds(start, band), :].astype(jnp.bfloat16))
            v_vals.append(v_refs[p][pl.ds(start, band), :].astype(jnp.bfloat16))
        else:
            k_vals.append(k_refs[p][...].astype(jnp.bfloat16))
            v_vals.append(v_refs[p][...].astype(jnp.bfloat16))
    for kk in range(KV_B):
        heads = [kk * G_B + g for g in range(G_B)]
        qs = jnp.concatenate([q[:, h * DH_B:(h + 1) * DH_B] for h in heads], axis=0)
        sk = jnp.zeros((G_B * tq, 1), jnp.float32)
        for g, h in enumerate(heads):
            sk = jnp.where(grp == g, sink[:, h:h + 1], sk)
        s_parts = []
        for p in range(n_parts):
            kc = k_vals[p][:, kk * DH_B:(kk + 1) * DH_B]
            s = lax.dot_general(qs, kc, (((1,), (1,)), ((), ())),
                                preferred_element_type=jnp.float32) * scale
            if band and p == n_parts - 1:
                s = jnp.where(valid, s, -1e30)
            s_parts.append(s)
        m = sk
        for s in s_parts:
            m = jnp.maximum(m, s.max(axis=-1, keepdims=True))
        e_parts = [jnp.exp(s - m) for s in s_parts]
        den = jnp.exp(sk - m)
        for e in e_parts:
            den = den + e.sum(axis=-1, keepdims=True)
        inv = 1.0 / den
        o = None
        for p in range(n_parts):
            pr = (e_parts[p] * inv).astype(jnp.bfloat16)
            contrib = jnp.dot(pr, v_vals[p][:, kk * DH_B:(kk + 1) * DH_B],
                              preferred_element_type=jnp.float32)
            o = contrib if o is None else o + contrib
        o_ref[:, kk * G_B * DH_B:(kk + 1) * G_B * DH_B] = jnp.concatenate(
            [o[g * tq:(g + 1) * tq] for g in range(G_B)], axis=1).astype(o_ref.dtype)


def _gqa_attention(q, k_parts, v_parts, sink, ib, *, n_batch, n_q, tq, band, out_row_start, prev_out):
    nqb = n_q // tq
    row_off = out_row_start // tq
    in_specs = [pl.BlockSpec((tq, D_MODEL), lambda b, i: (b * nqb + i, 0))]
    args = [q]
    for arr, spec in k_parts + v_parts:
        in_specs.append(spec)
        args.append(arr)
    in_specs.append(pl.BlockSpec((None, 1, H_B), lambda b, i: (ib, 0, 0)))
    args.append(sink.reshape(-1, 1, H_B))
    aliases = {}
    if prev_out is not None:
        in_specs.append(pl.BlockSpec(memory_space=pl.ANY))
        args.append(prev_out)
        aliases = {len(args) - 1: 0}

    def kern(*refs):
        if prev_out is not None:
            refs = refs[:-2] + refs[-1:]
        _gqa_kernel(*refs, n_parts=len(k_parts), band=band, tq=tq, n_lat=n_q)

    return pl.pallas_call(
        kern,
        out_shape=jax.ShapeDtypeStruct((T_ALL, D_MODEL), jnp.bfloat16),
        grid=(n_batch, nqb),
        in_specs=in_specs,
        out_specs=pl.BlockSpec((tq, D_MODEL), lambda b, i: (row_off + b * nqb + i, 0)),
        input_output_aliases=aliases,
        compiler_params=_cparams(("arbitrary", "arbitrary")),
        name="gqa_attn",
    )(*args)


def _pool_kernel(h_ref, w_ref, b_ref, ls_ref, o_ref, *, rows):
    i = pl.program_id(0)
    g = pl.program_id(1)
    seq = jnp.where(i * rows < T_P, SEQ, DEC_SEQ)
    pos = lax.broadcasted_iota(jnp.int32, (rows, 1), 0) & (seq - 1)

    def shifted(a, k):
        ok = (pos + k >= 0) & (pos + k < seq)
        return jnp.where(ok, pltpu.roll(a, (-k) % rows, 0), 0.0)

    for gi, wsz in enumerate(POOL_SIZES):
        @pl.when(g == gi)
        def _(wsz=wsz):
            h = h_ref[...].astype(jnp.float32)
            half = wsz // 2
            fwd = h
            m = 1
            while m < half:
                fwd = fwd + shifted(fwd, m)
                m *= 2
            bwd = shifted(h, -1)
            m = 1
            while m < half:
                bwd = bwd + shifted(bwd, -m)
                m *= 2
            lo = jnp.maximum(pos - half, 0)
            hi = jnp.minimum(pos + half - 1, seq - 1)
            cnt = (hi - lo + 1).astype(jnp.float32)
            diff = ((fwd + bwd) / cnt - h).astype(jnp.bfloat16)
            y = jnp.dot(diff, w_ref[...].astype(jnp.bfloat16), preferred_element_type=jnp.float32)
            o_ref[...] = (y + b_ref[...]) * ls_ref[...]


def _pool_mixer(h, pool_w, pool_b, pool_scale, ic):
    rows = DEC_SEQ
    ng = len(POOL_SIZES)
    return pl.pallas_call(
        functools.partial(_pool_kernel, rows=rows),
        out_shape=jax.ShapeDtypeStruct((T_ALL, D_MODEL), jnp.float32),
        grid=(T_ALL // rows, ng),
        in_specs=[
            pl.BlockSpec((rows, POOL_GROUP), lambda i, g: (i, g)),
            pl.BlockSpec((None, None, POOL_GROUP, POOL_GROUP), lambda i, g: (ic, g, 0, 0)),
            pl.BlockSpec((None, None, 1, POOL_GROUP), lambda i, g: (ic, g, 0, 0)),
            pl.BlockSpec((None, None, 1, POOL_GROUP), lambda i, g: (ic, g, 0, 0)),
        ],
        out_specs=pl.BlockSpec((rows, POOL_GROUP), lambda i, g: (i, g)),
        compiler_params=_cparams(("arbitrary", "arbitrary")),
        name="pool",
    )(h, pool_w, pool_b.reshape(-1, ng, 1, POOL_GROUP), pool_scale.reshape(-1, ng, 1, POOL_GROUP))


def _route_kernel(h_ref, w_ref, b_ref, g_ref, ls_ref, n_ref, off_ref, xs_ref):
    tt = h_ref.shape[0]
    h = h_ref[...]
    logits = jnp.dot(h, w_ref[...], precision=lax.Precision.HIGHEST,
                     preferred_element_type=jnp.float32) + b_ref[...]
    lane = lax.broadcasted_iota(jnp.int32, logits.shape, 1).astype(jnp.float32)
    work = logits
    vals, idxs = [], []
    member = jnp.zeros(logits.shape, jnp.float32)
    for _ in range(TOP_K):
        m = work.max(axis=-1, keepdims=True)
        idx = jnp.where(work == m, lane, float(N_EXPERTS)).min(axis=-1, keepdims=True)
        hit = lane == idx
        member = jnp.where(hit, 1.0, member)
        work = jnp.where(hit, -jnp.inf, work)
        vals.append(m)
        idxs.append(idx)
    es = [jnp.exp(v - vals[0]) for v in vals]
    den = es[0]
    for e in es[1:]:
        den = den + e
    r_i = lax.broadcasted_iota(jnp.int32, (tt, tt), 0)
    c_i = lax.broadcasted_iota(jnp.int32, (tt, tt), 1)
    tri = jnp.where(c_i < r_i, 1.0, 0.0).astype(jnp.bfloat16)
    before = jnp.dot(tri, member.astype(jnp.bfloat16), preferred_element_type=jnp.float32)
    n_units = jnp.floor((member.sum(axis=0, keepdims=True) + (MOE_ALIGN - 1)) * (1.0 / MOE_ALIGN))
    e_r = lax.broadcasted_iota(jnp.int32, (N_EXPERTS, N_EXPERTS), 0)
    e_c = lax.broadcasted_iota(jnp.int32, (N_EXPERTS, N_EXPERTS), 1)
    upper = jnp.where(e_r < e_c, 1.0, 0.0).astype(jnp.bfloat16)
    off_units = jnp.dot(jnp.broadcast_to(n_units, (8, N_EXPERTS)).astype(jnp.bfloat16), upper,
                        preferred_element_type=jnp.float32)[0:1]
    n_ref[...] = (n_units * MOE_ALIGN).astype(jnp.int32)
    off_ref[...] = (off_units * MOE_ALIGN).astype(jnp.int32)
    slot_of = off_units * MOE_ALIGN + before
    lane128 = lax.broadcasted_iota(jnp.int32, (tt, LANES), 1)
    ls_mat = jnp.full((tt, LANES), -1.0, jnp.float32)
    for k in range(TOP_K):
        ls_k = jnp.where(lane == idxs[k], slot_of, 0.0).sum(axis=-1, keepdims=True)
        g_ref[:, k:k + 1] = es[k] / den
        ls_ref[:, k:k + 1] = ls_k.astype(jnp.int32)
        ls_mat = jnp.where(lane128 == k, ls_k, ls_mat)
    ls_t = ls_mat.T
    hb = h.astype(jnp.bfloat16)
    for c in range(MOE_SMAX // tt):
        srow = (lax.broadcasted_iota(jnp.int32, (tt, tt), 0) + c * tt).astype(jnp.float32)
        sel = jnp.zeros((tt, tt), jnp.float32)
        for k in range(TOP_K):
            sel = jnp.where(srow == ls_t[k:k + 1, :], 1.0, sel)
        xs_ref[c * tt:(c + 1) * tt, :] = jnp.dot(
            sel.astype(jnp.bfloat16), hb, preferred_element_type=jnp.float32).astype(xs_ref.dtype)


def _route(h, w_router, b_router, layer):
    small = lambda dt: jax.ShapeDtypeStruct((T_ALL, TOP_K), dt)
    per_tile = jax.ShapeDtypeStruct((MOE_NT, 1, N_EXPERTS), jnp.int32)
    gates, ls, n_pad, off, xs = pl.pallas_call(
        _route_kernel,
        out_shape=[small(jnp.float32), small(jnp.int32), per_tile, per_tile,
                   jax.ShapeDtypeStruct((N_SLOTS, D_MODEL), jnp.bfloat16)],
        grid=(MOE_NT,),
        in_specs=[
            pl.BlockSpec((MOE_TT, D_MODEL), lambda i: (i, 0)),
            pl.BlockSpec((None, D_MODEL, N_EXPERTS), lambda i: (layer, 0, 0)),
            pl.BlockSpec((None, 1, N_EXPERTS), lambda i: (layer, 0, 0)),
        ],
        out_specs=[pl.BlockSpec((MOE_TT, TOP_K), lambda i: (i, 0))] * 2
        + [pl.BlockSpec((None, 1, N_EXPERTS), lambda i: (i, 0, 0))] * 2
        + [pl.BlockSpec((MOE_SMAX, D_MODEL), lambda i: (i, 0))],
        compiler_params=_cparams(("arbitrary",)),
        name="route",
    )(h, w_router, b_router.reshape(DEPTH, 1, N_EXPERTS))
    return gates, ls, n_pad.reshape(MOE_NT, N_EXPERTS), off.reshape(MOE_NT, N_EXPERTS), xs


def _ffn_kernel(sb_e, sb_i0, sb_i1, sb_nch, n_sb, g_src, g_n, g_dst,
                xs_ref, wg_ref, wu_ref, wd_ref, bg_ref, bu_ref, bd_ref,
                ys_ref, xbuf, yacc, wgb, wub, wdb, sem):
    del n_sb
    s = pl.program_id(0)
    j = pl.program_id(1)
    nf = pl.num_programs(1)
    nch = sb_nch[s]

    def for_pieces(fn):
        def body(i, carry):
            g = sb_e[s] * MOE_NT + i
            n = g_n[g]
            for size in DMA_SIZES:
                done = n & (-2 * size)

                @pl.when((n & size) != 0)
                def _(size=size, done=done):
                    fn(pl.multiple_of(g_src[g] + done, MOE_ALIGN),
                       pl.multiple_of(g_dst[g] + done, MOE_ALIGN), size)
            return carry
        lax.fori_loop(sb_i0[s], sb_i1[s], body, 0)

    def x_copy(hbm_row, vmem_row, size):
        return pltpu.make_async_copy(xs_ref.at[pl.ds(hbm_row, size)],
                                     xbuf.at[pl.ds(vmem_row, size)], sem.at[0])

    def y_copy(hbm_row, vmem_row, size):
        return pltpu.make_async_copy(xbuf.at[pl.ds(vmem_row, size)],
                                     ys_ref.at[pl.ds(hbm_row, size)], sem.at[1])

    def for_chunks(fn):
        def body(c, carry):
            fn(pl.ds(pl.multiple_of(c * MOE_CHUNK, MOE_CHUNK), MOE_CHUNK))
            return carry
        lax.fori_loop(0, nch, body, 0)

    @pl.when((j == 0) & (nch > 0))
    def _():
        xbuf[...] = jnp.zeros_like(xbuf)
        for_pieces(lambda *a: x_copy(*a).start())
        yacc[...] = jnp.broadcast_to(bd_ref[...], yacc.shape)
        for_pieces(lambda *a: x_copy(*a).wait())

    @pl.when(nch > 0)
    def _():
        wgb[...] = wg_ref[...].astype(jnp.bfloat16)
        wub[...] = wu_ref[...].astype(jnp.bfloat16)
        wdb[...] = wd_ref[...].astype(jnp.bfloat16)

        def chunk(rows):
            x = xbuf[rows, :]
            gate = jnp.dot(x, wgb[...], preferred_element_type=jnp.float32) + bg_ref[...]
            up = jnp.dot(x, wub[...], preferred_element_type=jnp.float32) + bu_ref[...]
            gate = jnp.minimum(gate, SWIGLU_LIMIT)
            up = jnp.clip(up, -SWIGLU_LIMIT, SWIGLU_LIMIT)
            act = (up + 1.0) * (gate / (1.0 + jnp.exp(-SWIGLU_ALPHA * gate)))
            yacc[rows, :] += jnp.dot(act.astype(jnp.bfloat16), wdb[...],
                                     preferred_element_type=jnp.float32)
        for_chunks(chunk)

    @pl.when((j == nf - 1) & (nch > 0))
    def _():
        def stage(rows):
            xbuf[rows, :] = yacc[rows, :].astype(xbuf.dtype)
        for_chunks(stage)
        for_pieces(lambda *a: y_copy(*a).start())
        for_pieces(lambda *a: y_copy(*a).wait())


N_FFN_PREFETCH = 8


def _expert_ffn(xs, tables, w_gu, b_gu, w_down, b_down, layer):
    nf = D_FF // MOE_TF

    def wspec(block, index):
        def index_map(s, j, sb_e, sb_i0, sb_i1, sb_nch, n_sb, *_):
            return index(sb_e[s], jnp.where(s < n_sb[0], j, nf - 1))
        return pl.BlockSpec(block, index_map)

    grid_spec = pltpu.PrefetchScalarGridSpec(
        num_scalar_prefetch=N_FFN_PREFETCH,
        grid=(N_SUPER, nf),
        in_specs=[
            pl.BlockSpec(memory_space=pl.ANY),
            wspec((None, None, D_MODEL, MOE_TF), lambda e, j: (layer, e, 0, j)),
            wspec((None, None, D_MODEL, MOE_TF), lambda e, j: (layer, e, 0, nf + j)),
            wspec((None, None, MOE_TF, D_MODEL), lambda e, j: (layer, e, j, 0)),
            wspec((None, None, 1, MOE_TF), lambda e, j: (layer, e, 0, j)),
            wspec((None, None, 1, MOE_TF), lambda e, j: (layer, e, 0, nf + j)),
            wspec((None, None, 1, D_MODEL), lambda e, j: (layer, e, 0, 0)),
        ],
        out_specs=pl.BlockSpec(memory_space=pl.ANY),
        scratch_shapes=[
            pltpu.VMEM((MOE_TMAX, D_MODEL), jnp.bfloat16),
            pltpu.VMEM((MOE_TMAX, D_MODEL), jnp.float32),
            pltpu.VMEM((D_MODEL, MOE_TF), jnp.bfloat16),
            pltpu.VMEM((D_MODEL, MOE_TF), jnp.bfloat16),
            pltpu.VMEM((MOE_TF, D_MODEL), jnp.bfloat16),
            pltpu.SemaphoreType.DMA((2,)),
        ],
    )
    b_gu4 = b_gu.reshape(DEPTH, N_EXPERTS, 1, 2 * D_FF)
    return pl.pallas_call(
        _ffn_kernel,
        out_shape=jax.ShapeDtypeStruct((N_SLOTS, D_MODEL), jnp.bfloat16),
        grid_spec=grid_spec,
        input_output_aliases={N_FFN_PREFETCH: 0},
        compiler_params=pltpu.CompilerParams(dimension_semantics=("arbitrary", "arbitrary"),
                                             vmem_limit_bytes=VMEM_LIMIT, has_side_effects=True),
        name="expert_ffn",
    )(*tables, xs, w_gu, w_gu, w_down, b_gu4, b_gu4, b_down.reshape(DEPTH, N_EXPERTS, 1, D_MODEL))


def _plan_superblocks(n_pad, off):
    i32 = jnp.int32
    cur = jnp.zeros((N_EXPERTS,), i32)
    n_closed = jnp.zeros((N_EXPERTS,), i32)
    local_sb, dst = [], []
    for i in range(MOE_NT):
        n = n_pad[i]
        overflow = cur + n > MOE_TMAX
        n_closed = n_closed + overflow.astype(i32)
        cur = jnp.where(overflow, n, cur + n)
        local_sb.append(n_closed)
        dst.append(cur - n)
    local_sb = jnp.stack(local_sb)
    dst = jnp.stack(dst)
    nsb_e = jnp.where(n_pad.sum(axis=0) > 0, n_closed + 1, 0)
    sb_end = jnp.cumsum(nsb_e)
    n_sb = sb_end[-1]
    group_sb = (sb_end - nsb_e)[None, :] + local_sb
    s = jnp.arange(N_SUPER, dtype=i32)
    e_of = jnp.minimum(jnp.searchsorted(sb_end, s, side='right'), N_EXPERTS - 1).astype(i32)
    live = s < n_sb
    mine = group_sb[:, e_of] == s[None, :]
    tile = jnp.arange(MOE_NT, dtype=i32)[:, None]
    i0 = jnp.min(jnp.where(mine, tile, MOE_NT), axis=0)
    i1 = jnp.max(jnp.where(mine, tile + 1, 0), axis=0)
    rows = jnp.sum(jnp.where(mine, n_pad[:, e_of], 0), axis=0)
    last_e = e_of[jnp.maximum(n_sb - 1, 0)]
    flat = lambda a: a.T.reshape(-1).astype(i32)
    return (jnp.where(live, e_of, last_e).astype(i32),
            jnp.where(live, i0, 0).astype(i32),
            jnp.where(live, i1, 0).astype(i32),
            jnp.where(live, (rows + MOE_CHUNK - 1) // MOE_CHUNK, 0).astype(i32),
            n_sb.reshape(1).astype(i32),
            flat(tile * MOE_SMAX + off), flat(n_pad), flat(dst))


def _combine_kernel(*refs, emit_h, gate_row, shift_row, scale_row):
    if emit_h:
        x_ref, ys_ref, ls_ref, g_ref, mod_ref, modn_ref, lng_ref, lnb_ref, xo_ref, h_ref = refs
    else:
        x_ref, ys_ref, ls_ref, g_ref, mod_ref, lng_ref, lnb_ref, xo_ref = refs
    ls = ls_ref[...]
    gts = g_ref[...]
    slot = lax.broadcasted_iota(jnp.int32, (ls.shape[0], MOE_SMAX), 1)
    weights = jnp.zeros(slot.shape, jnp.float32)
    for k in range(TOP_K):
        weights = jnp.where(slot == ls[:, k:k + 1], gts[:, k:k + 1], weights)
    sub = jnp.dot(weights.astype(jnp.bfloat16), ys_ref[...], preferred_element_type=jnp.float32)
    x = _post_norm(x_ref[...], sub, mod_ref[gate_row:gate_row + 1, :], lng_ref, lnb_ref)
    xo_ref[...] = x
    if emit_h:
        h_ref[...] = _modulate(x, modn_ref, shift_row, scale_row).astype(h_ref.dtype)


def _combine_norm(x, ys, ls, gates, ada4, layer, ln_g, ln_b, *, emit_h):
    tm = 256
    per = MOE_TT // tm
    row_spec = pl.BlockSpec((tm, D_MODEL), lambda i, r: (i * per + r, 0))
    small_spec = pl.BlockSpec((tm, TOP_K), lambda i, r: (i * per + r, 0))
    in_specs = [row_spec, pl.BlockSpec((MOE_SMAX, D_MODEL), lambda i, r: (i, 0)), small_spec, small_spec,
                _ada_spec(layer, tm, per)]
    args = [x, ys, ls, gates, ada4]
    out_shape = [jax.ShapeDtypeStruct((T_ALL, D_MODEL), jnp.float32)]
    out_specs = [row_spec]
    if emit_h:
        in_specs.append(_ada_spec(layer + 1, tm, per))
        args.append(ada4)
        out_shape.append(jax.ShapeDtypeStruct((T_ALL, D_MODEL), jnp.bfloat16))
        out_specs.append(row_spec)
    ln_specs, ln_arrs = _ln_args(ln_g, ln_b, layer, 1)
    kern = functools.partial(_combine_kernel, emit_h=emit_h, gate_row=5, shift_row=0, scale_row=1)
    return pl.pallas_call(
        kern, out_shape=out_shape, grid=(MOE_NT, per), in_specs=in_specs + ln_specs,
        out_specs=out_specs, compiler_params=_cparams(("arbitrary", "arbitrary")), name="combine",
    )(*args, *ln_arrs)


def kernel(x_prompt, x_sample, cache_diff_k, cache_diff_v, cache_win_k, cache_win_v, c, c_ctx, w_ada, b_ada, ln_g, ln_b, diff_wq, diff_wk, diff_wv, diff_wo, diff_lambda, diff_subln_g, win_wq, win_wk, win_wv, win_wo, win_sink, pool_w, pool_b, pool_scale, moe_w_router, moe_b_router, moe_w_gu, moe_b_gu, moe_w_down, moe_b_down):
    f32, bf16 = jnp.float32, jnp.bfloat16
    x = jnp.concatenate([x_prompt.reshape(T_P, D_MODEL), x_sample.reshape(T_S, D_MODEL)], axis=0)
    cvec = jnp.concatenate([c_ctx[None, :], c, jnp.zeros((COND_ROWS - N_COND, D_MODEL), f32)], axis=0)
    ada4 = _ada_all(cvec, w_ada, b_ada).reshape(DEPTH, COND_ROWS, N_ADA, D_MODEL)
    rope_a = _rope_tables(DK_A)
    rope_b = _rope_tables(DH_B)
    n_la = cache_diff_k.shape[1]
    n_lb = cache_win_k.shape[1]
    cdk = cache_diff_k.reshape(DEC_BATCH, n_la, PAST_LEN, D_MODEL)
    cdv = cache_diff_v.reshape(DEC_BATCH, n_la, PAST_LEN, D_MODEL)
    cwk = cache_win_k.reshape(DEC_BATCH, n_lb, PAST_LEN, KV_B * DH_B)
    cwv = cache_win_v.reshape(DEC_BATCH, n_lb, PAST_LEN, KV_B * DH_B)

    (h,) = _modnorm(x, None, ada4, 0, ln_g, ln_b, h_dtype=bf16, shift_row=0, scale_row=1)
    st_dk, st_dv, st_wk, st_wv = [], [], [], []
    ia = ib = ic = 0
    for i in range(DEPTH):
        kind = i % N_MIXERS
        if kind == 0:
            lam_init = 0.8 - 0.6 * math.exp(-0.3 * i)
            pj = functools.partial(_matmul, h, layer=ia)
            q_p = pj(diff_wq, row_start=0, n_rows=T_P, out_dtype=bf16)
            k_p = pj(diff_wk, row_start=0, n_rows=T_P, out_dtype=f32)
            v_p = pj(diff_wv, row_start=0, n_rows=T_P, out_dtype=f32)
            q_s = pj(diff_wq, row_start=T_P, n_rows=T_S, out_dtype=bf16, rope=rope_a)
            k_s = pj(diff_wk, row_start=T_P, n_rows=T_S, out_dtype=bf16, rope=rope_a)
            v_s = pj(diff_wv, row_start=T_P, n_rows=T_S, out_dtype=bf16)
            seq_spec = pl.BlockSpec((SEQ, DV_A), lambda b, hh, qi: (b, hh))
            o = _diff_attention(q_p, [(k_p, seq_spec)], [(v_p, seq_spec)], diff_lambda, diff_subln_g,
                                ia, lam_init, n_batch=BATCH, n_q=SEQ, tq=SEQ, out_row_start=0,
                                prev_out=None)
            cache_spec = pl.BlockSpec((None, None, PAST_LEN, DV_A), lambda b, hh, qi, ia=ia: (b, ia, 0, hh))
            lat_spec = pl.BlockSpec((DEC_SEQ, DV_A), lambda b, hh, qi: (b, hh))
            o = _diff_attention(q_s, [(cdk, cache_spec), (k_s, lat_spec)],
                                [(cdv, cache_spec), (v_s, lat_spec)], diff_lambda, diff_subln_g,
                                ia, lam_init, n_batch=DEC_BATCH, n_q=DEC_SEQ, tq=256,
                                out_row_start=T_P, prev_out=o)
            sub = _matmul(o, diff_wo, ia, row_start=0, n_rows=T_ALL, out_dtype=f32)
            st_dk.append(k_p.reshape(BATCH, SEQ, H_A, 2, DK_A))
            st_dv.append(v_p.reshape(BATCH, SEQ, H_A, DV_A))
            ia += 1
        elif kind == 1:
            pj = functools.partial(_matmul, h, layer=ib)
            q_p = pj(win_wq, row_start=0, n_rows=T_P, out_dtype=bf16)
            k_p = pj(win_wk, row_start=0, n_rows=T_P, out_dtype=f32)
            v_p = pj(win_wv, row_start=0, n_rows=T_P, out_dtype=f32)
            q_s = pj(win_wq, row_start=T_P, n_rows=T_S, out_dtype=bf16, rope=rope_b)
            k_s = pj(win_wk, row_start=T_P, n_rows=T_S, out_dtype=bf16, rope=rope_b)
            v_s = pj(win_wv, row_start=T_P, n_rows=T_S, out_dtype=bf16)
            kvw = KV_B * DH_B
            seq_spec = pl.BlockSpec((SEQ, kvw), lambda b, qi: (b, 0))
            o = _gqa_attention(q_p, [(k_p, seq_spec)], [(v_p, seq_spec)], win_sink, ib,
                               n_batch=BATCH, n_q=SEQ, tq=SEQ, band=0, out_row_start=0, prev_out=None)
            cache_spec = pl.BlockSpec((None, None, PAST_LEN, kvw), lambda b, qi, ib=ib: (b, ib, 0, 0))
            lat_spec = pl.BlockSpec((DEC_SEQ, kvw), lambda b, qi: (b, 0))
            o = _gqa_attention(q_s, [(cwk, cache_spec), (k_s, lat_spec)],
                               [(cwv, cache_spec), (v_s, lat_spec)], win_sink, ib,
                               n_batch=DEC_BATCH, n_q=DEC_SEQ, tq=WINDOW, band=3 * WINDOW,
                               out_row_start=T_P, prev_out=o)
            sub = _matmul(o, win_wo, ib, row_start=0, n_rows=T_ALL, out_dtype=f32)
            st_wk.append(k_p.reshape(BATCH, SEQ, KV_B, DH_B))
            st_wv.append(v_p.reshape(BATCH, SEQ, KV_B, DH_B))
            ib += 1
        else:
            sub = _pool_mixer(h, pool_w, pool_b, pool_scale, ic)
            ic += 1
        x, h = _modnorm(x, sub, ada4, i, ln_g, ln_b, h_dtype=f32, gate_row=2, shift_row=3, scale_row=4)
        gates, ls, n_pad, off, xs = _route(h, moe_w_router, moe_b_router, i)
        ys = _expert_ffn(xs, _plan_superblocks(n_pad, off), moe_w_gu, moe_b_gu, moe_w_down,
                         moe_b_down, i)
        outs = _combine_norm(x, ys, ls, gates, ada4, i, ln_g, ln_b, emit_h=i < DEPTH - 1)
        x = outs[0]
        if i < DEPTH - 1:
            h = outs[1]
    y_prompt = x[:T_P].reshape(BATCH, SEQ, D_MODEL)
    y_sample = x[T_P:].reshape(DEC_BATCH, DEC_SEQ, D_MODEL)
    return (y_prompt, y_sample, jnp.stack(st_dk, axis=1), jnp.stack(st_dv, axis=1),
            jnp.stack(st_wk, axis=1), jnp.stack(st_wv, axis=1))
```

```python
import functools
import math

import jax
import jax.numpy as jnp
from jax import lax
from jax.experimental import pallas as pl
from jax.experimental.pallas import tpu as pltpu

D_MODEL = 2048
BATCH = 16
SEQ = 256
DEPTH = 4
DEC_BATCH = 2
DEC_SEQ = 2048
PAST_LEN = 512
GRID_W = 64
N_MIXERS = 3
H_A = 8
DK_A = D_MODEL // (2 * H_A)
DV_A = 2 * DK_A
DH_B = 64
H_B = D_MODEL // DH_B
KV_B = 4
G_B = H_B // KV_B
WINDOW = 128
POOL_SIZES = (2, 4, 8, 16)
POOL_GROUP = D_MODEL // len(POOL_SIZES)
N_EXPERTS = 32
TOP_K = 4
D_FF = D_MODEL
SWIGLU_ALPHA = 1.702
SWIGLU_LIMIT = 7.0
ROPE_BASE = 10000.0
LN_EPS = 1e-5
DEEPNORM_ALPHA = (2 * DEPTH) ** 0.25
N_ADA = 6

T_P = BATCH * SEQ
T_S = DEC_BATCH * DEC_SEQ
T_ALL = T_P + T_S
N_COND = 1 + DEC_BATCH
COND_ROWS = 8

LANES = 128
VMEM_LIMIT = 52 * 1024 * 1024
BF16_ROWS = 16
MOE_TT = 512
MOE_NT = T_ALL // MOE_TT
MOE_ALIGN = BF16_ROWS
MOE_SMAX = MOE_TT * TOP_K + N_EXPERTS * MOE_ALIGN
N_SLOTS = MOE_NT * MOE_SMAX
MOE_CHUNK = 256
MOE_TMAX = 2048
MOE_TF = 256
N_SUPER = N_EXPERTS + N_SLOTS // (MOE_TMAX - MOE_TT)
DMA_SIZES = (512, 256, 128, 64, 32, 16)


def _cparams(sem, vmem=VMEM_LIMIT):
    return pltpu.CompilerParams(dimension_semantics=sem, vmem_limit_bytes=vmem)


def _cond_group(row):
    return jnp.maximum(row // DEC_SEQ - (T_P // DEC_SEQ - 1), 0)


def _ada_kernel(c_ref, w_ref, b_ref, o_ref):
    cv = c_ref[...]
    s = (cv / (1.0 + jnp.exp(-cv))).astype(jnp.bfloat16)
    o_ref[...] = jnp.dot(s, w_ref[...].astype(jnp.bfloat16),
                         preferred_element_type=jnp.float32) + b_ref[...]


def _ada_all(cvec, w_ada, b_ada):
    tn = 1024
    n = N_ADA * D_MODEL
    return pl.pallas_call(
        _ada_kernel,
        out_shape=jax.ShapeDtypeStruct((DEPTH, COND_ROWS, n), jnp.float32),
        grid=(DEPTH, n // tn),
        in_specs=[
            pl.BlockSpec((COND_ROWS, D_MODEL), lambda l, j: (0, 0)),
            pl.BlockSpec((None, D_MODEL, tn), lambda l, j: (l, 0, j)),
            pl.BlockSpec((None, 1, tn), lambda l, j: (l, 0, j)),
        ],
        out_specs=pl.BlockSpec((None, COND_ROWS, tn), lambda l, j: (l, 0, j)),
        compiler_params=_cparams(("arbitrary", "arbitrary")),
        name="ada",
    )(cvec, w_ada, b_ada.reshape(DEPTH, 1, n))


def _post_norm(x, sub, gate, lng_ref, lnb_ref):
    xf = DEEPNORM_ALPHA * x + gate * sub
    mu = jnp.mean(xf, axis=-1, keepdims=True)
    xc = xf - mu
    var = jnp.mean(xc * xc, axis=-1, keepdims=True)
    return xc * lax.rsqrt(var + LN_EPS) * lng_ref[...] + lnb_ref[...]


def _modulate(x, mod_ref, shift_row, scale_row):
    return x * (1.0 + mod_ref[scale_row:scale_row + 1, :]) + mod_ref[shift_row:shift_row + 1, :]


def _modnorm_kernel(*refs, do_norm, gate_row, shift_row, scale_row):
    if do_norm:
        x_ref, sub_ref, mod_ref, lng_ref, lnb_ref, xo_ref, h_ref = refs
        x = _post_norm(x_ref[...], sub_ref[...], mod_ref[gate_row:gate_row + 1, :], lng_ref, lnb_ref)
        xo_ref[...] = x
    else:
        x_ref, mod_ref, h_ref = refs
        x = x_ref[...]
    h_ref[...] = _modulate(x, mod_ref, shift_row, scale_row).astype(h_ref.dtype)


def _ada_spec(layer, tm, rows_per_step=1):
    return pl.BlockSpec((None, None, N_ADA, D_MODEL),
                        lambda i, *_: (layer, _cond_group(i * tm * rows_per_step), 0, 0))


def _ln_args(ln_g, ln_b, layer, which):
    spec = pl.BlockSpec((None, None, 1, D_MODEL), lambda i, *_: (layer, which, 0, 0))
    return [spec, spec], [ln_g.reshape(DEPTH, 2, 1, D_MODEL), ln_b.reshape(DEPTH, 2, 1, D_MODEL)]


def _modnorm(x, sub, ada4, layer, ln_g, ln_b, *, h_dtype, gate_row=0, shift_row=0, scale_row=0):
    tm = 256
    do_norm = sub is not None
    row_spec = pl.BlockSpec((tm, D_MODEL), lambda i: (i, 0))
    in_specs, args = [row_spec], [x]
    if do_norm:
        in_specs.append(row_spec)
        args.append(sub)
    in_specs.append(_ada_spec(layer, tm))
    args.append(ada4)
    out_shape = [jax.ShapeDtypeStruct((T_ALL, D_MODEL), h_dtype)]
    out_specs = [row_spec]
    if do_norm:
        ln_specs, ln_arrs = _ln_args(ln_g, ln_b, layer, 0)
        in_specs += ln_specs
        args += ln_arrs
        out_shape.insert(0, jax.ShapeDtypeStruct((T_ALL, D_MODEL), jnp.float32))
        out_specs.insert(0, row_spec)
    kern = functools.partial(_modnorm_kernel, do_norm=do_norm, gate_row=gate_row,
                             shift_row=shift_row, scale_row=scale_row)
    return pl.pallas_call(
        kern, out_shape=out_shape, grid=(T_ALL // tm,), in_specs=in_specs, out_specs=out_specs,
        compiler_params=_cparams(("arbitrary",)), name="modnorm",
    )(*args)


def _mm_kernel(*refs, rope_quarter, has_rope):
    if has_rope:
        a_ref, w_ref, cos_ref, sin_ref, o_ref, wb_ref = refs
    else:
        a_ref, w_ref, o_ref, wb_ref = refs

    @pl.when(pl.program_id(1) == 0)
    def _():
        wb_ref[...] = w_ref[...].astype(jnp.bfloat16)

    acc = jnp.dot(a_ref[...], wb_ref[...], preferred_element_type=jnp.float32)
    if has_rope:
        q = rope_quarter
        cs = cos_ref[...]
        sn = sin_ref[...]
        lane = lax.broadcasted_iota(jnp.int32, (acc.shape[0], LANES), 1)
        first = (lane % (2 * q)) < q
        for c in range(acc.shape[1] // LANES):
            a = acc[:, c * LANES:(c + 1) * LANES]
            partner = jnp.where(first, pltpu.roll(a, LANES - q, 1), pltpu.roll(a, q, 1))
            o_ref[:, c * LANES:(c + 1) * LANES] = (a * cs + partner * sn).astype(o_ref.dtype)
    else:
        o_ref[...] = acc.astype(o_ref.dtype)


def _matmul(a, w, layer, *, row_start, n_rows, out_dtype, rope=None, tm=1024, tn=512):
    k = a.shape[1]
    n = w.shape[2]
    tn = min(tn, n)
    m_off = row_start // tm
    in_specs = [
        pl.BlockSpec((tm, k), lambda j, i: (i + m_off, 0)),
        pl.BlockSpec((None, k, tn), lambda j, i: (layer, 0, j)),
    ]
    args = [a, w]
    has_rope = rope is not None
    quarter = 0
    if has_rope:
        cos_t, sin_t, quarter = rope
        per = DEC_SEQ // tm
        in_specs += [pl.BlockSpec((tm, LANES), lambda j, i: (i % per, 0))] * 2
        args += [cos_t, sin_t]
    return pl.pallas_call(
        functools.partial(_mm_kernel, rope_quarter=quarter, has_rope=has_rope),
        out_shape=jax.ShapeDtypeStruct((n_rows, n), out_dtype),
        grid=(n // tn, n_rows // tm),
        in_specs=in_specs,
        out_specs=pl.BlockSpec((tm, tn), lambda j, i: (i, j)),
        scratch_shapes=[pltpu.VMEM((k, tn), jnp.bfloat16)],
        compiler_params=_cparams(("arbitrary", "arbitrary")),
        name="proj",
    )(*args)


def _rope_tables(head_dim):
    quarter = head_dim // 4
    pos = jnp.arange(DEC_SEQ)
    row = (pos // GRID_W).astype(jnp.float32)
    col = (pos % GRID_W).astype(jnp.float32)
    inv = ROPE_BASE ** (-jnp.arange(quarter, dtype=jnp.float32) / quarter)
    ang_r = row[:, None] * inv
    ang_c = col[:, None] * inv
    ang = jnp.concatenate([ang_r, ang_r, ang_c, ang_c], axis=1)
    sign = jnp.concatenate([-jnp.ones(quarter), jnp.ones(quarter)] * 2)
    reps = LANES // head_dim
    cos_t = jnp.tile(jnp.cos(ang), (1, reps))
    sin_t = jnp.tile(jnp.sin(ang) * sign[None, :], (1, reps))
    return cos_t.astype(jnp.float32), sin_t.astype(jnp.float32), quarter


def _diff_attn_kernel(*refs, n_parts, lam_init):
    q_ref = refs[0]
    k_refs = refs[1:1 + n_parts]
    v_refs = refs[1 + n_parts:1 + 2 * n_parts]
    lam_ref, g_ref, o_ref = refs[1 + 2 * n_parts:]
    scale = DK_A ** -0.5
    lp = lam_ref[...]
    lam = (jnp.exp(jnp.sum(lp[0:1] * lp[1:2], axis=-1, keepdims=True))
           - jnp.exp(jnp.sum(lp[2:3] * lp[3:4], axis=-1, keepdims=True)) + lam_init)
    q = q_ref[...]
    v_vals = [vr[...].astype(jnp.bfloat16) for vr in v_refs]
    outs = []
    for c in range(2):
        qc = q[:, c * DK_A:(c + 1) * DK_A]
        s_parts = []
        for kr in k_refs:
            kc = kr[:, c * DK_A:(c + 1) * DK_A].astype(jnp.bfloat16)
            s_parts.append(lax.dot_general(qc, kc, (((1,), (1,)), ((), ())),
                                           preferred_element_type=jnp.float32) * scale)
        m = s_parts[0].max(axis=-1, keepdims=True)
        for s in s_parts[1:]:
            m = jnp.maximum(m, s.max(axis=-1, keepdims=True))
        e_parts = [jnp.exp(s - m) for s in s_parts]
        den = e_parts[0].sum(axis=-1, keepdims=True)
        for e in e_parts[1:]:
            den = den + e.sum(axis=-1, keepdims=True)
        oc = None
        for e, v in zip(e_parts, v_vals):
            contrib = jnp.dot(e.astype(jnp.bfloat16), v, preferred_element_type=jnp.float32)
            oc = contrib if oc is None else oc + contrib
        outs.append(oc / den)
    o = outs[0] - lam * outs[1]
    ms = jnp.mean(o * o, axis=-1, keepdims=True)
    o = o * lax.rsqrt(ms + LN_EPS) * g_ref[...] * (1.0 - lam_init)
    o_ref[...] = o.astype(o_ref.dtype)


def _diff_attention(q, k_parts, v_parts, lam_p, subln_g, ia, lam_init, *, n_batch, n_q, tq,
                    out_row_start, prev_out):
    nqb = n_q // tq
    row_off = out_row_start // tq
    in_specs = [pl.BlockSpec((tq, DV_A), lambda b, h, i: (b * nqb + i, h))]
    args = [q]
    for arr, spec in k_parts + v_parts:
        in_specs.append(spec)
        args.append(arr)
    in_specs.append(pl.BlockSpec((None, 4, DK_A), lambda b, h, i: (ia, 0, 0)))
    in_specs.append(pl.BlockSpec((None, 1, DV_A), lambda b, h, i: (ia, 0, 0)))
    args += [lam_p, subln_g.reshape(-1, 1, DV_A)]
    aliases = {}
    if prev_out is not None:
        in_specs.append(pl.BlockSpec(memory_space=pl.ANY))
        args.append(prev_out)
        aliases = {len(args) - 1: 0}

    def kern(*refs):
        if prev_out is not None:
            refs = refs[:-2] + refs[-1:]
        _diff_attn_kernel(*refs, n_parts=len(k_parts), lam_init=lam_init)

    return pl.pallas_call(
        kern,
        out_shape=jax.ShapeDtypeStruct((T_ALL, D_MODEL), jnp.bfloat16),
        grid=(n_batch, H_A, nqb),
        in_specs=in_specs,
        out_specs=pl.BlockSpec((tq, DV_A), lambda b, h, i: (row_off + b * nqb + i, h)),
        input_output_aliases=aliases,
        compiler_params=_cparams(("arbitrary", "arbitrary", "arbitrary")),
        name="diff_attn",
    )(*args)


def _gqa_kernel(*refs, n_parts, band, tq, n_lat):
    q_ref = refs[0]
    k_refs = refs[1:1 + n_parts]
    v_refs = refs[1 + n_parts:1 + 2 * n_parts]
    sink_ref, o_ref = refs[1 + 2 * n_parts:]
    scale = DH_B ** -0.5
    i = pl.program_id(1)
    q = q_ref[...]
    sink = sink_ref[...]
    grp = lax.broadcasted_iota(jnp.int32, (G_B * tq, 1), 0) // tq
    if band:
        start = pl.multiple_of(jnp.clip(i * tq - WINDOW, 0, n_lat - band), WINDOW)
        qpos = i * tq + (lax.broadcasted_iota(jnp.int32, (G_B * tq, band), 0) & (tq - 1))
        kpos = start + lax.broadcasted_iota(jnp.int32, (G_B * tq, band), 1)
        valid = jnp.abs(qpos - kpos) <= WINDOW
    k_vals, v_vals = [], []
    for p in range(n_parts):
        if band and p == n_parts - 1:
            k_vals.append(k_refs[p][pl.ds(start, band), :].astype(jnp.bfloat16))
            v_vals.append(v_refs[p][pl.ds(start, band), :].astype(jnp.bfloat16))
        else:
            k_vals.append(k_refs[p][...].astype(jnp.bfloat16))
            v_vals.append(v_refs[p][...].astype(jnp.bfloat16))
    for kk in range(KV_B):
        heads = [kk * G_B + g for g in range(G_B)]
        qs = jnp.concatenate([q[:, h * DH_B:(h + 1) * DH_B] for h in heads], axis=0)
        sk = jnp.zeros((G_B * tq, 1), jnp.float32)
        for g, h in enumerate(heads):
            sk = jnp.where(grp == g, sink[:, h:h + 1], sk)
        s_parts = []
        for p in range(n_parts):
            kc = k_vals[p][:, kk * DH_B:(kk + 1) * DH_B]
            s = lax.dot_general(qs, kc, (((1,), (1,)), ((), ())),
                                preferred_element_type=jnp.float32) * scale
            if band and p == n_parts - 1:
                s = jnp.where(valid, s, -1e30)
            s_parts.append(s)
        m = sk
        for s in s_parts:
            m = jnp.maximum(m, s.max(axis=-1, keepdims=True))
        e_parts = [jnp.exp(s - m) for s in s_parts]
        den = jnp.exp(sk - m)
        for e in e_parts:
            den = den + e.sum(axis=-1, keepdims=True)
        inv = 1.0 / den
        o = None
        for p in range(n_parts):
            pr = (e_parts[p] * inv).astype(jnp.bfloat16)
            contrib = jnp.dot(pr, v_vals[p][:, kk * DH_B:(kk + 1) * DH_B],
                              preferred_element_type=jnp.float32)
            o = contrib if o is None else o + contrib
        o_ref[:, kk * G_B * DH_B:(kk + 1) * G_B * DH_B] = jnp.concatenate(
            [o[g * tq:(g + 1) * tq] for g in range(G_B)], axis=1).astype(o_ref.dtype)


def _gqa_attention(q, k_parts, v_parts, sink, ib, *, n_batch, n_q, tq, band, out_row_start, prev_out):
    nqb = n_q // tq
    row_off = out_row_start // tq
    in_specs = [pl.BlockSpec((tq, D_MODEL), lambda b, i: (b * nqb + i, 0))]
    args = [q]
    for arr, spec in k_parts + v_parts:
        in_specs.append(spec)
        args.append(arr)
    in_specs.append(pl.BlockSpec((None, 1, H_B), lambda b, i: (ib, 0, 0)))
    args.append(sink.reshape(-1, 1, H_B))
    aliases = {}
    if prev_out is not None:
        in_specs.append(pl.BlockSpec(memory_space=pl.ANY))
        args.append(prev_out)
        aliases = {len(args) - 1: 0}

    def kern(*refs):
        if prev_out is not None:
            refs = refs[:-2] + refs[-1:]
        _gqa_kernel(*refs, n_parts=len(k_parts), band=band, tq=tq, n_lat=n_q)

    return pl.pallas_call(
        kern,
        out_shape=jax.ShapeDtypeStruct((T_ALL, D_MODEL), jnp.bfloat16),
        grid=(n_batch, nqb),
        in_specs=in_specs,
        out_specs=pl.BlockSpec((tq, D_MODEL), lambda b, i: (row_off + b * nqb + i, 0)),
        input_output_aliases=aliases,
        compiler_params=_cparams(("arbitrary", "arbitrary")),
        name="gqa_attn",
    )(*args)


def _pool_kernel(h_ref, w_ref, b_ref, ls_ref, o_ref, *, rows):
    i = pl.program_id(0)
    g = pl.program_id(1)
    seq = jnp.where(i * rows < T_P, SEQ, DEC_SEQ)
    pos = lax.broadcasted_iota(jnp.int32, (rows, 1), 0) & (seq - 1)

    def shifted(a, k):
        ok = (pos + k >= 0) & (pos + k < seq)
        return jnp.where(ok, pltpu.roll(a, (-k) % rows, 0), 0.0)

    for gi, wsz in enumerate(POOL_SIZES):
        @pl.when(g == gi)
        def _(wsz=wsz):
            h = h_ref[...].astype(jnp.float32)
            half = wsz // 2
            fwd = h
            m = 1
            while m < half:
                fwd = fwd + shifted(fwd, m)
                m *= 2
            bwd = shifted(h, -1)
            m = 1
            while m < half:
                bwd = bwd + shifted(bwd, -m)
                m *= 2
            lo = jnp.maximum(pos - half, 0)
            hi = jnp.minimum(pos + half - 1, seq - 1)
            cnt = (hi - lo + 1).astype(jnp.float32)
            diff = ((fwd + bwd) / cnt - h).astype(jnp.bfloat16)
            y = jnp.dot(diff, w_ref[...].astype(jnp.bfloat16), preferred_element_type=jnp.float32)
            o_ref[...] = (y + b_ref[...]) * ls_ref[...]


def _pool_mixer(h, pool_w, pool_b, pool_scale, ic):
    rows = DEC_SEQ
    ng = len(POOL_SIZES)
    return pl.pallas_call(
        functools.partial(_pool_kernel, rows=rows),
        out_shape=jax.ShapeDtypeStruct((T_ALL, D_MODEL), jnp.float32),
        grid=(T_ALL // rows, ng),
        in_specs=[
            pl.BlockSpec((rows, POOL_GROUP), lambda i, g: (i, g)),
            pl.BlockSpec((None, None, POOL_GROUP, POOL_GROUP), lambda i, g: (ic, g, 0, 0)),
            pl.BlockSpec((None, None, 1, POOL_GROUP), lambda i, g: (ic, g, 0, 0)),
            pl.BlockSpec((None, None, 1, POOL_GROUP), lambda i, g: (ic, g, 0, 0)),
        ],
        out_specs=pl.BlockSpec((rows, POOL_GROUP), lambda i, g: (i, g)),
        compiler_params=_cparams(("arbitrary", "arbitrary")),
        name="pool",
    )(h, pool_w, pool_b.reshape(-1, ng, 1, POOL_GROUP), pool_scale.reshape(-1, ng, 1, POOL_GROUP))


def _route_kernel(h_ref, w_ref, b_ref, g_ref, ls_ref, n_ref, off_ref, xs_ref):
    tt = h_ref.shape[0]
    h = h_ref[...]
    logits = jnp.dot(h, w_ref[...], precision=lax.Precision.HIGHEST,
                     preferred_element_type=jnp.float32) + b_ref[...]
    lane = lax.broadcasted_iota(jnp.int32, logits.shape, 1).astype(jnp.float32)
    work = logits
    vals, idxs = [], []
    member = jnp.zeros(logits.shape, jnp.float32)
    for _ in range(TOP_K):
        m = work.max(axis=-1, keepdims=True)
        idx = jnp.where(work == m, lane, float(N_EXPERTS)).min(axis=-1, keepdims=True)
        hit = lane == idx
        member = jnp.where(hit, 1.0, member)
        work = jnp.where(hit, -jnp.inf, work)
        vals.append(m)
        idxs.append(idx)
    es = [jnp.exp(v - vals[0]) for v in vals]
    den = es[0]
    for e in es[1:]:
        den = den + e
    r_i = lax.broadcasted_iota(jnp.int32, (tt, tt), 0)
    c_i = lax.broadcasted_iota(jnp.int32, (tt, tt), 1)
    tri = jnp.where(c_i < r_i, 1.0, 0.0).astype(jnp.bfloat16)
    before = jnp.dot(tri, member.astype(jnp.bfloat16), preferred_element_type=jnp.float32)
    n_units = jnp.floor((member.sum(axis=0, keepdims=True) + (MOE_ALIGN - 1)) * (1.0 / MOE_ALIGN))
    e_r = lax.broadcasted_iota(jnp.int32, (N_EXPERTS, N_EXPERTS), 0)
    e_c = lax.broadcasted_iota(jnp.int32, (N_EXPERTS, N_EXPERTS), 1)
    upper = jnp.where(e_r < e_c, 1.0, 0.0).astype(jnp.bfloat16)
    off_units = jnp.dot(jnp.broadcast_to(n_units, (8, N_EXPERTS)).astype(jnp.bfloat16), upper,
                        preferred_element_type=jnp.float32)[0:1]
    n_ref[...] = (n_units * MOE_ALIGN).astype(jnp.int32)
    off_ref[...] = (off_units * MOE_ALIGN).astype(jnp.int32)
    slot_of = off_units * MOE_ALIGN + before
    lane128 = lax.broadcasted_iota(jnp.int32, (tt, LANES), 1)
    ls_mat = jnp.full((tt, LANES), -1.0, jnp.float32)
    for k in range(TOP_K):
        ls_k = jnp.where(lane == idxs[k], slot_of, 0.0).sum(axis=-1, keepdims=True)
        g_ref[:, k:k + 1] = es[k] / den
        ls_ref[:, k:k + 1] = ls_k.astype(jnp.int32)
        ls_mat = jnp.where(lane128 == k, ls_k, ls_mat)
    ls_t = ls_mat.T
    hb = h.astype(jnp.bfloat16)
    for c in range(MOE_SMAX // tt):
        srow = (lax.broadcasted_iota(jnp.int32, (tt, tt), 0) + c * tt).astype(jnp.float32)
        sel = jnp.zeros((tt, tt), jnp.float32)
        for k in range(TOP_K):
            sel = jnp.where(srow == ls_t[k:k + 1, :], 1.0, sel)
        xs_ref[c * tt:(c + 1) * tt, :] = jnp.dot(
            sel.astype(jnp.bfloat16), hb, preferred_element_type=jnp.float32).astype(xs_ref.dtype)


def _route(h, w_router, b_router, layer):
    small = lambda dt: jax.ShapeDtypeStruct((T_ALL, TOP_K), dt)
    per_tile = jax.ShapeDtypeStruct((MOE_NT, 1, N_EXPERTS), jnp.int32)
    gates, ls, n_pad, off, xs = pl.pallas_call(
        _route_kernel,
        out_shape=[small(jnp.float32), small(jnp.int32), per_tile, per_tile,
                   jax.ShapeDtypeStruct((N_SLOTS, D_MODEL), jnp.bfloat16)],
        grid=(MOE_NT,),
        in_specs=[
            pl.BlockSpec((MOE_TT, D_MODEL), lambda i: (i, 0)),
            pl.BlockSpec((None, D_MODEL, N_EXPERTS), lambda i: (layer, 0, 0)),
            pl.BlockSpec((None, 1, N_EXPERTS), lambda i: (layer, 0, 0)),
        ],
        out_specs=[pl.BlockSpec((MOE_TT, TOP_K), lambda i: (i, 0))] * 2
        + [pl.BlockSpec((None, 1, N_EXPERTS), lambda i: (i, 0, 0))] * 2
        + [pl.BlockSpec((MOE_SMAX, D_MODEL), lambda i: (i, 0))],
        compiler_params=_cparams(("arbitrary",)),
        name="route",
    )(h, w_router, b_router.reshape(DEPTH, 1, N_EXPERTS))
    return gates, ls, n_pad.reshape(MOE_NT, N_EXPERTS), off.reshape(MOE_NT, N_EXPERTS), xs


def _ffn_kernel(sb_e, sb_i0, sb_i1, sb_nch, n_sb, g_src, g_n, g_dst,
                xs_ref, wg_ref, wu_ref, wd_ref, bg_ref, bu_ref, bd_ref,
                ys_ref, xbuf, yacc, wgb, wub, wdb, sem):
    del n_sb
    s = pl.program_id(0)
    j = pl.program_id(1)
    nf = pl.num_programs(1)
    nch = sb_nch[s]

    def for_pieces(fn):
        def body(i, carry):
            g = sb_e[s] * MOE_NT + i
            n = g_n[g]
            for size in DMA_SIZES:
                done = n & (-2 * size)

                @pl.when((n & size) != 0)
                def _(size=size, done=done):
                    fn(pl.multiple_of(g_src[g] + done, MOE_ALIGN),
                       pl.multiple_of(g_dst[g] + done, MOE_ALIGN), size)
            return carry
        lax.fori_loop(sb_i0[s], sb_i1[s], body, 0)

    def x_copy(hbm_row, vmem_row, size):
        return pltpu.make_async_copy(xs_ref.at[pl.ds(hbm_row, size)],
                                     xbuf.at[pl.ds(vmem_row, size)], sem.at[0])

    def y_copy(hbm_row, vmem_row, size):
        return pltpu.make_async_copy(xbuf.at[pl.ds(vmem_row, size)],
                                     ys_ref.at[pl.ds(hbm_row, size)], sem.at[1])

    def chunk_rows(c):
        return pl.ds(pl.multiple_of(c * MOE_CHUNK, MOE_CHUNK), MOE_CHUNK)

    def for_chunks(fn):
        def body(c, carry):
            fn(chunk_rows(c))
            return carry
        lax.fori_loop(0, nch, body, 0)

    @pl.when((j == 0) & (nch > 0))
    def _():
        def clear(rows):
            xbuf[rows, :] = jnp.zeros((MOE_CHUNK, D_MODEL), xbuf.dtype)
        for_chunks(clear)
        for_pieces(lambda *a: x_copy(*a).start())

        def init(rows):
            yacc[rows, :] = jnp.broadcast_to(bd_ref[...], (MOE_CHUNK, D_MODEL))
        for_chunks(init)
        for_pieces(lambda *a: x_copy(*a).wait())

    @pl.when(nch > 0)
    def _():
        wgb[...] = wg_ref[...].astype(jnp.bfloat16)
        wub[...] = wu_ref[...].astype(jnp.bfloat16)
        wdb[...] = wd_ref[...].astype(jnp.bfloat16)

        def gate_up(rows):
            x = xbuf[rows, :]
            gate = jnp.dot(x, wgb[...], preferred_element_type=jnp.float32) + bg_ref[...]
            up = jnp.dot(x, wub[...], preferred_element_type=jnp.float32) + bu_ref[...]
            return gate, up

        def down(rows, gate, up):
            gate = jnp.minimum(gate, SWIGLU_LIMIT)
            up = jnp.clip(up, -SWIGLU_LIMIT, SWIGLU_LIMIT)
            act = (up + 1.0) * (gate / (1.0 + jnp.exp(-SWIGLU_ALPHA * gate)))
            yacc[rows, :] += jnp.dot(act.astype(jnp.bfloat16), wdb[...],
                                     preferred_element_type=jnp.float32)

        def pair(p, carry):
            r0, r1 = chunk_rows(2 * p), chunk_rows(2 * p + 1)
            gu0 = gate_up(r0)
            gu1 = gate_up(r1)
            down(r0, *gu0)
            down(r1, *gu1)
            return carry
        lax.fori_loop(0, nch // 2, pair, 0)

        @pl.when(nch % 2 == 1)
        def _():
            rows = chunk_rows(nch - 1)
            down(rows, *gate_up(rows))

    @pl.when((j == nf - 1) & (nch > 0))
    def _():
        def stage(rows):
            xbuf[rows, :] = yacc[rows, :].astype(xbuf.dtype)
        for_chunks(stage)
        for_pieces(lambda *a: y_copy(*a).start())
        for_pieces(lambda *a: y_copy(*a).wait())


N_FFN_PREFETCH = 8


def _expert_ffn(xs, tables, w_gu, b_gu, w_down, b_down, layer):
    nf = D_FF // MOE_TF

    def wspec(block, index):
        def index_map(s, j, sb_e, sb_i0, sb_i1, sb_nch, n_sb, *_):
            return index(sb_e[s], jnp.where(s < n_sb[0], j, nf - 1))
        return pl.BlockSpec(block, index_map)

    grid_spec = pltpu.PrefetchScalarGridSpec(
        num_scalar_prefetch=N_FFN_PREFETCH,
        grid=(N_SUPER, nf),
        in_specs=[
            pl.BlockSpec(memory_space=pl.ANY),
            wspec((None, None, D_MODEL, MOE_TF), lambda e, j: (layer, e, 0, j)),
            wspec((None, None, D_MODEL, MOE_TF), lambda e, j: (layer, e, 0, nf + j)),
            wspec((None, None, MOE_TF, D_MODEL), lambda e, j: (layer, e, j, 0)),
            wspec((None, None, 1, MOE_TF), lambda e, j: (layer, e, 0, j)),
            wspec((None, None, 1, MOE_TF), lambda e, j: (layer, e, 0, nf + j)),
            wspec((None, None, 1, D_MODEL), lambda e, j: (layer, e, 0, 0)),
        ],
        out_specs=pl.BlockSpec(memory_space=pl.ANY),
        scratch_shapes=[
            pltpu.VMEM((MOE_TMAX, D_MODEL), jnp.bfloat16),
            pltpu.VMEM((MOE_TMAX, D_MODEL), jnp.float32),
            pltpu.VMEM((D_MODEL, MOE_TF), jnp.bfloat16),
            pltpu.VMEM((D_MODEL, MOE_TF), jnp.bfloat16),
            pltpu.VMEM((MOE_TF, D_MODEL), jnp.bfloat16),
            pltpu.SemaphoreType.DMA((2,)),
        ],
    )
    b_gu4 = b_gu.reshape(DEPTH, N_EXPERTS, 1, 2 * D_FF)
    return pl.pallas_call(
        _ffn_kernel,
        out_shape=jax.ShapeDtypeStruct((N_SLOTS, D_MODEL), jnp.bfloat16),
        grid_spec=grid_spec,
        input_output_aliases={N_FFN_PREFETCH: 0},
        compiler_params=pltpu.CompilerParams(dimension_semantics=("arbitrary", "arbitrary"),
                                             vmem_limit_bytes=VMEM_LIMIT, has_side_effects=True),
        name="expert_ffn",
    )(*tables, xs, w_gu, w_gu, w_down, b_gu4, b_gu4, b_down.reshape(DEPTH, N_EXPERTS, 1, D_MODEL))


def _plan_superblocks(n_pad, off):
    i32 = jnp.int32
    cur = jnp.zeros((N_EXPERTS,), i32)
    n_closed = jnp.zeros((N_EXPERTS,), i32)
    local_sb, dst = [], []
    for i in range(MOE_NT):
        n = n_pad[i]
        overflow = cur + n > MOE_TMAX
        n_closed = n_closed + overflow.astype(i32)
        cur = jnp.where(overflow, n, cur + n)
        local_sb.append(n_closed)
        dst.append(cur - n)
    local_sb = jnp.stack(local_sb)
    dst = jnp.stack(dst)
    nsb_e = jnp.where(n_pad.sum(axis=0) > 0, n_closed + 1, 0)
    sb_end = jnp.cumsum(nsb_e)
    n_sb = sb_end[-1]
    group_sb = (sb_end - nsb_e)[None, :] + local_sb
    s = jnp.arange(N_SUPER, dtype=i32)
    e_of = jnp.minimum(jnp.searchsorted(sb_end, s, side='right'), N_EXPERTS - 1).astype(i32)
    live = s < n_sb
    mine = group_sb[:, e_of] == s[None, :]
    tile = jnp.arange(MOE_NT, dtype=i32)[:, None]
    i0 = jnp.min(jnp.where(mine, tile, MOE_NT), axis=0)
    i1 = jnp.max(jnp.where(mine, tile + 1, 0), axis=0)
    rows = jnp.sum(jnp.where(mine, n_pad[:, e_of], 0), axis=0)
    last_e = e_of[jnp.maximum(n_sb - 1, 0)]
    flat = lambda a: a.T.reshape(-1).astype(i32)
    return (jnp.where(live, e_of, last_e).astype(i32),
            jnp.where(live, i0, 0).astype(i32),
            jnp.where(live, i1, 0).astype(i32),
            jnp.where(live, (rows + MOE_CHUNK - 1) // MOE_CHUNK, 0).astype(i32),
            n_sb.reshape(1).astype(i32),
            flat(tile * MOE_SMAX + off), flat(n_pad), flat(dst))


def _combine_kernel(*refs, emit_h, gate_row, shift_row, scale_row):
    if emit_h:
        x_ref, ys_ref, ls_ref, g_ref, mod_ref, modn_ref, lng_ref, lnb_ref, xo_ref, h_ref = refs
    else:
        x_ref, ys_ref, ls_ref, g_ref, mod_ref, lng_ref, lnb_ref, xo_ref = refs
    ls = ls_ref[...]
    gts = g_ref[...]
    slot = lax.broadcasted_iota(jnp.int32, (ls.shape[0], MOE_SMAX), 1)
    weights = jnp.zeros(slot.shape, jnp.float32)
    for k in range(TOP_K):
        weights = jnp.where(slot == ls[:, k:k + 1], gts[:, k:k + 1], weights)
    sub = jnp.dot(weights.astype(jnp.bfloat16), ys_ref[...], preferred_element_type=jnp.float32)
    x = _post_norm(x_ref[...], sub, mod_ref[gate_row:gate_row + 1, :], lng_ref, lnb_ref)
    xo_ref[...] = x
    if emit_h:
        h_ref[...] = _modulate(x, modn_ref, shift_row, scale_row).astype(h_ref.dtype)


def _combine_norm(x, ys, ls, gates, ada4, layer, ln_g, ln_b, *, emit_h):
    tm = 256
    per = MOE_TT // tm
    row_spec = pl.BlockSpec((tm, D_MODEL), lambda i, r: (i * per + r, 0))
    small_spec = pl.BlockSpec((tm, TOP_K), lambda i, r: (i * per + r, 0))
    in_specs = [row_spec, pl.BlockSpec((MOE_SMAX, D_MODEL), lambda i, r: (i, 0)), small_spec, small_spec,
                _ada_spec(layer, tm, per)]
    args = [x, ys, ls, gates, ada4]
    out_shape = [jax.ShapeDtypeStruct((T_ALL, D_MODEL), jnp.float32)]
    out_specs = [row_spec]
    if emit_h:
        in_specs.append(_ada_spec(layer + 1, tm, per))
        args.append(ada4)
        out_shape.append(jax.ShapeDtypeStruct((T_ALL, D_MODEL), jnp.bfloat16))
        out_specs.append(row_spec)
    ln_specs, ln_arrs = _ln_args(ln_g, ln_b, layer, 1)
    kern = functools.partial(_combine_kernel, emit_h=emit_h, gate_row=5, shift_row=0, scale_row=1)
    return pl.pallas_call(
        kern, out_shape=out_shape, grid=(MOE_NT, per), in_specs=in_specs + ln_specs,
        out_specs=out_specs, compiler_params=_cparams(("arbitrary", "arbitrary")), name="combine",
    )(*args, *ln_arrs)


def kernel(x_prompt, x_sample, cache_diff_k, cache_diff_v, cache_win_k, cache_win_v, c, c_ctx, w_ada, b_ada, ln_g, ln_b, diff_wq, diff_wk, diff_wv, diff_wo, diff_lambda, diff_subln_g, win_wq, win_wk, win_wv, win_wo, win_sink, pool_w, pool_b, pool_scale, moe_w_router, moe_b_router, moe_w_gu, moe_b_gu, moe_w_down, moe_b_down):
    f32, bf16 = jnp.float32, jnp.bfloat16
    x = jnp.concatenate([x_prompt.reshape(T_P, D_MODEL), x_sample.reshape(T_S, D_MODEL)], axis=0)
    cvec = jnp.concatenate([c_ctx[None, :], c, jnp.zeros((COND_ROWS - N_COND, D_MODEL), f32)], axis=0)
    ada4 = _ada_all(cvec, w_ada, b_ada).reshape(DEPTH, COND_ROWS, N_ADA, D_MODEL)
    rope_a = _rope_tables(DK_A)
    rope_b = _rope_tables(DH_B)
    n_la = cache_diff_k.shape[1]
    n_lb = cache_win_k.shape[1]
    cdk = cache_diff_k.reshape(DEC_BATCH, n_la, PAST_LEN, D_MODEL)
    cdv = cache_diff_v.reshape(DEC_BATCH, n_la, PAST_LEN, D_MODEL)
    cwk = cache_win_k.reshape(DEC_BATCH, n_lb, PAST_LEN, KV_B * DH_B)
    cwv = cache_win_v.reshape(DEC_BATCH, n_lb, PAST_LEN, KV_B * DH_B)

    (h,) = _modnorm(x, None, ada4, 0, ln_g, ln_b, h_dtype=bf16, shift_row=0, scale_row=1)
    st_dk, st_dv, st_wk, st_wv = [], [], [], []
    ia = ib = ic = 0
    for i in range(DEPTH):
        kind = i % N_MIXERS
        if kind == 0:
            lam_init = 0.8 - 0.6 * math.exp(-0.3 * i)
            pj = functools.partial(_matmul, h, layer=ia)
            q_p = pj(diff_wq, row_start=0, n_rows=T_P, out_dtype=bf16)
            k_p = pj(diff_wk, row_start=0, n_rows=T_P, out_dtype=f32)
            v_p = pj(diff_wv, row_start=0, n_rows=T_P, out_dtype=f32)
            q_s = pj(diff_wq, row_start=T_P, n_rows=T_S, out_dtype=bf16, rope=rope_a)
            k_s = pj(diff_wk, row_start=T_P, n_rows=T_S, out_dtype=bf16, rope=rope_a)
            v_s = pj(diff_wv, row_start=T_P, n_rows=T_S, out_dtype=bf16)
            seq_spec = pl.BlockSpec((SEQ, DV_A), lambda b, hh, qi: (b, hh))
            o = _diff_attention(q_p, [(k_p, seq_spec)], [(v_p, seq_spec)], diff_lambda, diff_subln_g,
                                ia, lam_init, n_batch=BATCH, n_q=SEQ, tq=SEQ, out_row_start=0,
                                prev_out=None)
            cache_spec = pl.BlockSpec((None, None, PAST_LEN, DV_A), lambda b, hh, qi, ia=ia: (b, ia, 0, hh))
            lat_spec = pl.BlockSpec((DEC_SEQ, DV_A), lambda b, hh, qi: (b, hh))
            o = _diff_attention(q_s, [(cdk, cache_spec), (k_s, lat_spec)],
                                [(cdv, cache_spec), (v_s, lat_spec)], diff_lambda, diff_subln_g,
                                ia, lam_init, n_batch=DEC_BATCH, n_q=DEC_SEQ, tq=256,
                                out_row_start=T_P, prev_out=o)
            sub = _matmul(o, diff_wo, ia, row_start=0, n_rows=T_ALL, out_dtype=f32)
            st_dk.append(k_p.reshape(BATCH, SEQ, H_A, 2, DK_A))
            st_dv.append(v_p.reshape(BATCH, SEQ, H_A, DV_A))
            ia += 1
        elif kind == 1:
            pj = functools.partial(_matmul, h, layer=ib)
            q_p = pj(win_wq, row_start=0, n_rows=T_P, out_dtype=bf16)
            k_p = pj(win_wk, row_start=0, n_rows=T_P, out_dtype=f32)
            v_p = pj(win_wv, row_start=0, n_rows=T_P, out_dtype=f32)
            q_s = pj(win_wq, row_start=T_P, n_rows=T_S, out_dtype=bf16, rope=rope_b)
            k_s = pj(win_wk, row_start=T_P, n_rows=T_S, out_dtype=bf16, rope=rope_b)
            v_s = pj(win_wv, row_start=T_P, n_rows=T_S, out_dtype=bf16)
            kvw = KV_B * DH_B
            seq_spec = pl.BlockSpec((SEQ, kvw), lambda b, qi: (b, 0))
            o = _gqa_attention(q_p, [(k_p, seq_spec)], [(v_p, seq_spec)], win_sink, ib,
                               n_batch=BATCH, n_q=SEQ, tq=SEQ, band=0, out_row_start=0, prev_out=None)
            cache_spec = pl.BlockSpec((None, None, PAST_LEN, kvw), lambda b, qi, ib=ib: (b, ib, 0, 0))
            lat_spec = pl.BlockSpec((DEC_SEQ, kvw), lambda b, qi: (b, 0))
            o = _gqa_attention(q_s, [(cwk, cache_spec), (k_s, lat_spec)],
                               [(cwv, cache_spec), (v_s, lat_spec)], win_sink, ib,
                               n_batch=DEC_BATCH, n_q=DEC_SEQ, tq=WINDOW, band=3 * WINDOW,
                               out_row_start=T_P, prev_out=o)
            sub = _matmul(o, win_wo, ib, row_start=0, n_rows=T_ALL, out_dtype=f32)
            st_wk.append(k_p.reshape(BATCH, SEQ, KV_B, DH_B))
            st_wv.append(v_p.reshape(BATCH, SEQ, KV_B, DH_B))
            ib += 1
        else:
            sub = _pool_mixer(h, pool_w, pool_b, pool_scale, ic)
            ic += 1
        x, h = _modnorm(x, sub, ada4, i, ln_g, ln_b, h_dtype=f32, gate_row=2, shift_row=3, scale_row=4)
        gates, ls, n_pad, off, xs = _route(h, moe_w_router, moe_b_router, i)
        ys = _expert_ffn(xs, _plan_superblocks(n_pad, off), moe_w_gu, moe_b_gu, moe_w_down,
                         moe_b_down, i)
        outs = _combine_norm(x, ys, ls, gates, ada4, i, ln_g, ln_b, emit_h=i < DEPTH - 1)
        x = outs[0]
        if i < DEPTH - 1:
            h = outs[1]
    y_prompt = x[:T_P].reshape(BATCH, SEQ, D_MODEL)
    y_sample = x[T_P:].reshape(DEC_BATCH, DEC_SEQ, D_MODEL)
    return (y_prompt, y_sample, jnp.stack(st_dk, axis=1), jnp.stack(st_dv, axis=1),
            jnp.stack(st_wk, axis=1), jnp.stack(st_wv, axis=1))
```

```python
import functools
import math

import jax
import jax.numpy as jnp
from jax import lax
from jax.experimental import pallas as pl
from jax.experimental.pallas import tpu as pltpu

D_MODEL = 2048
BATCH = 16
SEQ = 256
DEPTH = 4
DEC_BATCH = 2
DEC_SEQ = 2048
PAST_LEN = 512
GRID_W = 64
N_MIXERS = 3
H_A = 8
DK_A = D_MODEL // (2 * H_A)
DV_A = 2 * DK_A
DH_B = 64
H_B = D_MODEL // DH_B
KV_B = 4
G_B = H_B // KV_B
WINDOW = 128
POOL_SIZES = (2, 4, 8, 16)
POOL_GROUP = D_MODEL // len(POOL_SIZES)
N_EXPERTS = 32
TOP_K = 4
D_FF = D_MODEL
SWIGLU_ALPHA = 1.702
SWIGLU_LIMIT = 7.0
ROPE_BASE = 10000.0
LN_EPS = 1e-5
DEEPNORM_ALPHA = (2 * DEPTH) ** 0.25
N_ADA = 6

T_P = BATCH * SEQ
T_S = DEC_BATCH * DEC_SEQ
T_ALL = T_P + T_S
N_COND = 1 + DEC_BATCH
COND_ROWS = 8

LANES = 128
VMEM_LIMIT = 52 * 1024 * 1024
BF16_ROWS = 16
MOE_TT = 512
MOE_NT = T_ALL // MOE_TT
MOE_ALIGN = BF16_ROWS
MOE_SMAX = MOE_TT * TOP_K + N_EXPERTS * MOE_ALIGN
N_SLOTS = MOE_NT * MOE_SMAX
MOE_CHUNK = 256
MOE_HALF = MOE_CHUNK // 2
MOE_TMAX = 2048
MOE_TF = 256
N_SUPER = N_EXPERTS + N_SLOTS // (MOE_TMAX - MOE_TT)
DMA_SIZES = (512, 256, 128, 64, 32, 16)


def _cparams(sem, vmem=VMEM_LIMIT):
    return pltpu.CompilerParams(dimension_semantics=sem, vmem_limit_bytes=vmem)


def _cond_group(row):
    return jnp.maximum(row // DEC_SEQ - (T_P // DEC_SEQ - 1), 0)


def _ada_kernel(c_ref, w_ref, b_ref, o_ref):
    cv = c_ref[...]
    s = (cv / (1.0 + jnp.exp(-cv))).astype(jnp.bfloat16)
    o_ref[...] = jnp.dot(s, w_ref[...].astype(jnp.bfloat16),
                         preferred_element_type=jnp.float32) + b_ref[...]


def _ada_all(cvec, w_ada, b_ada):
    tn = 1024
    n = N_ADA * D_MODEL
    return pl.pallas_call(
        _ada_kernel,
        out_shape=jax.ShapeDtypeStruct((DEPTH, COND_ROWS, n), jnp.float32),
        grid=(DEPTH, n // tn),
        in_specs=[
            pl.BlockSpec((COND_ROWS, D_MODEL), lambda l, j: (0, 0)),
            pl.BlockSpec((None, D_MODEL, tn), lambda l, j: (l, 0, j)),
            pl.BlockSpec((None, 1, tn), lambda l, j: (l, 0, j)),
        ],
        out_specs=pl.BlockSpec((None, COND_ROWS, tn), lambda l, j: (l, 0, j)),
        compiler_params=_cparams(("arbitrary", "arbitrary")),
        name="ada",
    )(cvec, w_ada, b_ada.reshape(DEPTH, 1, n))


def _post_norm(x, sub, gate, lng_ref, lnb_ref):
    xf = DEEPNORM_ALPHA * x + gate * sub
    mu = jnp.mean(xf, axis=-1, keepdims=True)
    xc = xf - mu
    var = jnp.mean(xc * xc, axis=-1, keepdims=True)
    return xc * lax.rsqrt(var + LN_EPS) * lng_ref[...] + lnb_ref[...]


def _modulate(x, mod_ref, shift_row, scale_row):
    return x * (1.0 + mod_ref[scale_row:scale_row + 1, :]) + mod_ref[shift_row:shift_row + 1, :]


def _modnorm_kernel(*refs, do_norm, gate_row, shift_row, scale_row):
    if do_norm:
        x_ref, sub_ref, mod_ref, lng_ref, lnb_ref, xo_ref, h_ref = refs
        x = _post_norm(x_ref[...], sub_ref[...], mod_ref[gate_row:gate_row + 1, :], lng_ref, lnb_ref)
        xo_ref[...] = x
    else:
        x_ref, mod_ref, h_ref = refs
        x = x_ref[...]
    h_ref[...] = _modulate(x, mod_ref, shift_row, scale_row).astype(h_ref.dtype)


def _ada_spec(layer, tm, rows_per_step=1):
    return pl.BlockSpec((None, None, N_ADA, D_MODEL),
                        lambda i, *_: (layer, _cond_group(i * tm * rows_per_step), 0, 0))


def _ln_args(ln_g, ln_b, layer, which):
    spec = pl.BlockSpec((None, None, 1, D_MODEL), lambda i, *_: (layer, which, 0, 0))
    return [spec, spec], [ln_g.reshape(DEPTH, 2, 1, D_MODEL), ln_b.reshape(DEPTH, 2, 1, D_MODEL)]


def _modnorm(x, sub, ada4, layer, ln_g, ln_b, *, h_dtype, gate_row=0, shift_row=0, scale_row=0):
    tm = 256
    do_norm = sub is not None
    row_spec = pl.BlockSpec((tm, D_MODEL), lambda i: (i, 0))
    in_specs, args = [row_spec], [x]
    if do_norm:
        in_specs.append(row_spec)
        args.append(sub)
    in_specs.append(_ada_spec(layer, tm))
    args.append(ada4)
    out_shape = [jax.ShapeDtypeStruct((T_ALL, D_MODEL), h_dtype)]
    out_specs = [row_spec]
    if do_norm:
        ln_specs, ln_arrs = _ln_args(ln_g, ln_b, layer, 0)
        in_specs += ln_specs
        args += ln_arrs
        out_shape.insert(0, jax.ShapeDtypeStruct((T_ALL, D_MODEL), jnp.float32))
        out_specs.insert(0, row_spec)
    kern = functools.partial(_modnorm_kernel, do_norm=do_norm, gate_row=gate_row,
                             shift_row=shift_row, scale_row=scale_row)
    return pl.pallas_call(
        kern, out_shape=out_shape, grid=(T_ALL // tm,), in_specs=in_specs, out_specs=out_specs,
        compiler_params=_cparams(("arbitrary",)), name="modnorm",
    )(*args)


def _mm_kernel(*refs, rope_quarter, has_rope):
    if has_rope:
        a_ref, w_ref, cos_ref, sin_ref, o_ref, wb_ref = refs
    else:
        a_ref, w_ref, o_ref, wb_ref = refs

    @pl.when(pl.program_id(1) == 0)
    def _():
        wb_ref[...] = w_ref[...].astype(jnp.bfloat16)

    acc = jnp.dot(a_ref[...], wb_ref[...], preferred_element_type=jnp.float32)
    if has_rope:
        q = rope_quarter
        cs = cos_ref[...]
        sn = sin_ref[...]
        lane = lax.broadcasted_iota(jnp.int32, (acc.shape[0], LANES), 1)
        first = (lane % (2 * q)) < q
        for c in range(acc.shape[1] // LANES):
            a = acc[:, c * LANES:(c + 1) * LANES]
            partner = jnp.where(first, pltpu.roll(a, LANES - q, 1), pltpu.roll(a, q, 1))
            o_ref[:, c * LANES:(c + 1) * LANES] = (a * cs + partner * sn).astype(o_ref.dtype)
    else:
        o_ref[...] = acc.astype(o_ref.dtype)


def _matmul(a, w, layer, *, row_start, n_rows, out_dtype, rope=None, tm=1024, tn=512):
    k = a.shape[1]
    n = w.shape[2]
    tn = min(tn, n)
    m_off = row_start // tm
    in_specs = [
        pl.BlockSpec((tm, k), lambda j, i: (i + m_off, 0)),
        pl.BlockSpec((None, k, tn), lambda j, i: (layer, 0, j)),
    ]
    args = [a, w]
    has_rope = rope is not None
    quarter = 0
    if has_rope:
        cos_t, sin_t, quarter = rope
        per = DEC_SEQ // tm
        in_specs += [pl.BlockSpec((tm, LANES), lambda j, i: (i % per, 0))] * 2
        args += [cos_t, sin_t]
    return pl.pallas_call(
        functools.partial(_mm_kernel, rope_quarter=quarter, has_rope=has_rope),
        out_shape=jax.ShapeDtypeStruct((n_rows, n), out_dtype),
        grid=(n // tn, n_rows // tm),
        in_specs=in_specs,
        out_specs=pl.BlockSpec((tm, tn), lambda j, i: (i, j)),
        scratch_shapes=[pltpu.VMEM((k, tn), jnp.bfloat16)],
        compiler_params=_cparams(("arbitrary", "arbitrary")),
        name="proj",
    )(*args)


def _rope_tables(head_dim):
    quarter = head_dim // 4
    pos = jnp.arange(DEC_SEQ)
    row = (pos // GRID_W).astype(jnp.float32)
    col = (pos % GRID_W).astype(jnp.float32)
    inv = ROPE_BASE ** (-jnp.arange(quarter, dtype=jnp.float32) / quarter)
    ang_r = row[:, None] * inv
    ang_c = col[:, None] * inv
    ang = jnp.concatenate([ang_r, ang_r, ang_c, ang_c], axis=1)
    sign = jnp.concatenate([-jnp.ones(quarter), jnp.ones(quarter)] * 2)
    reps = LANES // head_dim
    cos_t = jnp.tile(jnp.cos(ang), (1, reps))
    sin_t = jnp.tile(jnp.sin(ang) * sign[None, :], (1, reps))
    return cos_t.astype(jnp.float32), sin_t.astype(jnp.float32), quarter


def _diff_attn_kernel(*refs, n_parts, lam_init):
    q_ref = refs[0]
    k_refs = refs[1:1 + n_parts]
    v_refs = refs[1 + n_parts:1 + 2 * n_parts]
    lam_ref, g_ref, o_ref = refs[1 + 2 * n_parts:]
    scale = DK_A ** -0.5
    lp = lam_ref[...]
    lam = (jnp.exp(jnp.sum(lp[0:1] * lp[1:2], axis=-1, keepdims=True))
           - jnp.exp(jnp.sum(lp[2:3] * lp[3:4], axis=-1, keepdims=True)) + lam_init)
    q = q_ref[...]
    v_vals = [vr[...].astype(jnp.bfloat16) for vr in v_refs]
    outs = []
    for c in range(2):
        qc = q[:, c * DK_A:(c + 1) * DK_A]
        s_parts = []
        for kr in k_refs:
            kc = kr[:, c * DK_A:(c + 1) * DK_A].astype(jnp.bfloat16)
            s_parts.append(lax.dot_general(qc, kc, (((1,), (1,)), ((), ())),
                                           preferred_element_type=jnp.float32) * scale)
        m = s_parts[0].max(axis=-1, keepdims=True)
        for s in s_parts[1:]:
            m = jnp.maximum(m, s.max(axis=-1, keepdims=True))
        e_parts = [jnp.exp(s - m) for s in s_parts]
        den = e_parts[0].sum(axis=-1, keepdims=True)
        for e in e_parts[1:]:
            den = den + e.sum(axis=-1, keepdims=True)
        oc = None
        for e, v in zip(e_parts, v_vals):
            contrib = jnp.dot(e.astype(jnp.bfloat16), v, preferred_element_type=jnp.float32)
            oc = contrib if oc is None else oc + contrib
        outs.append(oc / den)
    o = outs[0] - lam * outs[1]
    ms = jnp.mean(o * o, axis=-1, keepdims=True)
    o = o * lax.rsqrt(ms + LN_EPS) * g_ref[...] * (1.0 - lam_init)
    o_ref[...] = o.astype(o_ref.dtype)


def _diff_attention(q, k_parts, v_parts, lam_p, subln_g, ia, lam_init, *, n_batch, n_q, tq,
                    out_row_start, prev_out):
    nqb = n_q // tq
    row_off = out_row_start // tq
    in_specs = [pl.BlockSpec((tq, DV_A), lambda b, h, i: (b * nqb + i, h))]
    args = [q]
    for arr, spec in k_parts + v_parts:
        in_specs.append(spec)
        args.append(arr)
    in_specs.append(pl.BlockSpec((None, 4, DK_A), lambda b, h, i: (ia, 0, 0)))
    in_specs.append(pl.BlockSpec((None, 1, DV_A), lambda b, h, i: (ia, 0, 0)))
    args += [lam_p, subln_g.reshape(-1, 1, DV_A)]
    aliases = {}
    if prev_out is not None:
        in_specs.append(pl.BlockSpec(memory_space=pl.ANY))
        args.append(prev_out)
        aliases = {len(args) - 1: 0}

    def kern(*refs):
        if prev_out is not None:
            refs = refs[:-2] + refs[-1:]
        _diff_attn_kernel(*refs, n_parts=len(k_parts), lam_init=lam_init)

    return pl.pallas_call(
        kern,
        out_shape=jax.ShapeDtypeStruct((T_ALL, D_MODEL), jnp.bfloat16),
        grid=(n_batch, H_A, nqb),
        in_specs=in_specs,
        out_specs=pl.BlockSpec((tq, DV_A), lambda b, h, i: (row_off + b * nqb + i, h)),
        input_output_aliases=aliases,
        compiler_params=_cparams(("arbitrary", "arbitrary", "arbitrary")),
        name="diff_attn",
    )(*args)


def _gqa_kernel(*refs, n_parts, band, tq, n_lat):
    q_ref = refs[0]
    k_refs = refs[1:1 + n_parts]
    v_refs = refs[1 + n_parts:1 + 2 * n_parts]
    sink_ref, o_ref = refs[1 + 2 * n_parts:]
    scale = DH_B ** -0.5
    i = pl.program_id(1)
    q = q_ref[...]
    sink = sink_ref[...]
    grp = lax.broadcasted_iota(jnp.int32, (G_B * tq, 1), 0) // tq
    if band:
        start = pl.multiple_of(jnp.clip(i * tq - WINDOW, 0, n_lat - band), WINDOW)
        qpos = i * tq + (lax.broadcasted_iota(jnp.int32, (G_B * tq, band), 0) & (tq - 1))
        kpos = start + lax.broadcasted_iota(jnp.int32, (G_B * tq, band), 1)
        valid = jnp.abs(qpos - kpos) <= WINDOW
    k_vals, v_vals = [], []
    for p in range(n_parts):
        if band and p == n_parts - 1:
            k_vals.append(k_refs[p][pl.ds(start, band), :].astype(jnp.bfloat16))
            v_vals.append(v_refs[p][pl.ds(start, band), :].astype(jnp.bfloat16))
        else:
            k_vals.append(k_refs[p][...].astype(jnp.bfloat16))
            v_vals.append(v_refs[p][...].astype(jnp.bfloat16))
    for kk in range(KV_B):
        heads = [kk * G_B + g for g in range(G_B)]
        qs = jnp.concatenate([q[:, h * DH_B:(h + 1) * DH_B] for h in heads], axis=0)
        sk = jnp.zeros((G_B * tq, 1), jnp.float32)
        for g, h in enumerate(heads):
            sk = jnp.where(grp == g, sink[:, h:h + 1], sk)
        s_parts = []
        for p in range(n_parts):
            kc = k_vals[p][:, kk * DH_B:(kk + 1) * DH_B]
            s = lax.dot_general(qs, kc, (((1,), (1,)), ((), ())),
                                preferred_element_type=jnp.float32) * scale
            if band and p == n_parts - 1:
                s = jnp.where(valid, s, -1e30)
            s_parts.append(s)
        m = sk
        for s in s_parts:
            m = jnp.maximum(m, s.max(axis=-1, keepdims=True))
        e_parts = [jnp.exp(s - m) for s in s_parts]
        den = jnp.exp(sk - m)
        for e in e_parts:
            den = den + e.sum(axis=-1, keepdims=True)
        inv = 1.0 / den
        o = None
        for p in range(n_parts):
            pr = (e_parts[p] * inv).astype(jnp.bfloat16)
            contrib = jnp.dot(pr, v_vals[p][:, kk * DH_B:(kk + 1) * DH_B],
                              preferred_element_type=jnp.float32)
            o = contrib if o is None else o + contrib
        o_ref[:, kk * G_B * DH_B:(kk + 1) * G_B * DH_B] = jnp.concatenate(
            [o[g * tq:(g + 1) * tq] for g in range(G_B)], axis=1).astype(o_ref.dtype)


def _gqa_attention(q, k_parts, v_parts, sink, ib, *, n_batch, n_q, tq, band, out_row_start, prev_out):
    nqb = n_q // tq
    row_off = out_row_start // tq
    in_specs = [pl.BlockSpec((tq, D_MODEL), lambda b, i: (b * nqb + i, 0))]
    args = [q]
    for arr, spec in k_parts + v_parts:
        in_specs.append(spec)
        args.append(arr)
    in_specs.append(pl.BlockSpec((None, 1, H_B), lambda b, i: (ib, 0, 0)))
    args.append(sink.reshape(-1, 1, H_B))
    aliases = {}
    if prev_out is not None:
        in_specs.append(pl.BlockSpec(memory_space=pl.ANY))
        args.append(prev_out)
        aliases = {len(args) - 1: 0}

    def kern(*refs):
        if prev_out is not None:
            refs = refs[:-2] + refs[-1:]
        _gqa_kernel(*refs, n_parts=len(k_parts), band=band, tq=tq, n_lat=n_q)

    return pl.pallas_call(
        kern,
        out_shape=jax.ShapeDtypeStruct((T_ALL, D_MODEL), jnp.bfloat16),
        grid=(n_batch, nqb),
        in_specs=in_specs,
        out_specs=pl.BlockSpec((tq, D_MODEL), lambda b, i: (row_off + b * nqb + i, 0)),
        input_output_aliases=aliases,
        compiler_params=_cparams(("arbitrary", "arbitrary")),
        name="gqa_attn",
    )(*args)


def _pool_kernel(h_ref, w_ref, b_ref, ls_ref, o_ref, *, rows):
    i = pl.program_id(0)
    g = pl.program_id(1)
    seq = jnp.where(i * rows < T_P, SEQ, DEC_SEQ)
    pos = lax.broadcasted_iota(jnp.int32, (rows, 1), 0) & (seq - 1)

    def shifted(a, k):
        ok = (pos + k >= 0) & (pos + k < seq)
        return jnp.where(ok, pltpu.roll(a, (-k) % rows, 0), 0.0)

    for gi, wsz in enumerate(POOL_SIZES):
        @pl.when(g == gi)
        def _(wsz=wsz):
            h = h_ref[...].astype(jnp.float32)
            half = wsz // 2
            fwd = h
            m = 1
            while m < half:
                fwd = fwd + shifted(fwd, m)
                m *= 2
            bwd = shifted(h, -1)
            m = 1
            while m < half:
                bwd = bwd + shifted(bwd, -m)
                m *= 2
            lo = jnp.maximum(pos - half, 0)
            hi = jnp.minimum(pos + half - 1, seq - 1)
            cnt = (hi - lo + 1).astype(jnp.float32)
            diff = ((fwd + bwd) / cnt - h).astype(jnp.bfloat16)
            y = jnp.dot(diff, w_ref[...].astype(jnp.bfloat16), preferred_element_type=jnp.float32)
            o_ref[...] = (y + b_ref[...]) * ls_ref[...]


def _pool_mixer(h, pool_w, pool_b, pool_scale, ic):
    rows = DEC_SEQ
    ng = len(POOL_SIZES)
    return pl.pallas_call(
        functools.partial(_pool_kernel, rows=rows),
        out_shape=jax.ShapeDtypeStruct((T_ALL, D_MODEL), jnp.float32),
        grid=(T_ALL // rows, ng),
        in_specs=[
            pl.BlockSpec((rows, POOL_GROUP), lambda i, g: (i, g)),
            pl.BlockSpec((None, None, POOL_GROUP, POOL_GROUP), lambda i, g: (ic, g, 0, 0)),
            pl.BlockSpec((None, None, 1, POOL_GROUP), lambda i, g: (ic, g, 0, 0)),
            pl.BlockSpec((None, None, 1, POOL_GROUP), lambda i, g: (ic, g, 0, 0)),
        ],
        out_specs=pl.BlockSpec((rows, POOL_GROUP), lambda i, g: (i, g)),
        compiler_params=_cparams(("arbitrary", "arbitrary")),
        name="pool",
    )(h, pool_w, pool_b.reshape(-1, ng, 1, POOL_GROUP), pool_scale.reshape(-1, ng, 1, POOL_GROUP))


def _route_kernel(h_ref, w_ref, b_ref, g_ref, ls_ref, n_ref, off_ref, xs_ref):
    tt = h_ref.shape[0]
    h = h_ref[...]
    logits = jnp.dot(h, w_ref[...], precision=lax.Precision.HIGHEST,
                     preferred_element_type=jnp.float32) + b_ref[...]
    lane = lax.broadcasted_iota(jnp.int32, logits.shape, 1).astype(jnp.float32)
    work = logits
    vals, idxs = [], []
    member = jnp.zeros(logits.shape, jnp.float32)
    for _ in range(TOP_K):
        m = work.max(axis=-1, keepdims=True)
        idx = jnp.where(work == m, lane, float(N_EXPERTS)).min(axis=-1, keepdims=True)
        hit = lane == idx
        member = jnp.where(hit, 1.0, member)
        work = jnp.where(hit, -jnp.inf, work)
        vals.append(m)
        idxs.append(idx)
    es = [jnp.exp(v - vals[0]) for v in vals]
    den = es[0]
    for e in es[1:]:
        den = den + e
    r_i = lax.broadcasted_iota(jnp.int32, (tt, tt), 0)
    c_i = lax.broadcasted_iota(jnp.int32, (tt, tt), 1)
    tri = jnp.where(c_i < r_i, 1.0, 0.0).astype(jnp.bfloat16)
    before = jnp.dot(tri, member.astype(jnp.bfloat16), preferred_element_type=jnp.float32)
    n_units = jnp.floor((member.sum(axis=0, keepdims=True) + (MOE_ALIGN - 1)) * (1.0 / MOE_ALIGN))
    e_r = lax.broadcasted_iota(jnp.int32, (N_EXPERTS, N_EXPERTS), 0)
    e_c = lax.broadcasted_iota(jnp.int32, (N_EXPERTS, N_EXPERTS), 1)
    upper = jnp.where(e_r < e_c, 1.0, 0.0).astype(jnp.bfloat16)
    off_units = jnp.dot(jnp.broadcast_to(n_units, (8, N_EXPERTS)).astype(jnp.bfloat16), upper,
                        preferred_element_type=jnp.float32)[0:1]
    n_ref[...] = (n_units * MOE_ALIGN).astype(jnp.int32)
    off_ref[...] = (off_units * MOE_ALIGN).astype(jnp.int32)
    slot_of = off_units * MOE_ALIGN + before
    lane128 = lax.broadcasted_iota(jnp.int32, (tt, LANES), 1)
    ls_mat = jnp.full((tt, LANES), -1.0, jnp.float32)
    for k in range(TOP_K):
        ls_k = jnp.where(lane == idxs[k], slot_of, 0.0).sum(axis=-1, keepdims=True)
        g_ref[:, k:k + 1] = es[k] / den
        ls_ref[:, k:k + 1] = ls_k.astype(jnp.int32)
        ls_mat = jnp.where(lane128 == k, ls_k, ls_mat)
    ls_t = ls_mat.T
    hb = h.astype(jnp.bfloat16)
    for c in range(MOE_SMAX // tt):
        srow = (lax.broadcasted_iota(jnp.int32, (tt, tt), 0) + c * tt).astype(jnp.float32)
        sel = jnp.zeros((tt, tt), jnp.float32)
        for k in range(TOP_K):
            sel = jnp.where(srow == ls_t[k:k + 1, :], 1.0, sel)
        xs_ref[c * tt:(c + 1) * tt, :] = jnp.dot(
            sel.astype(jnp.bfloat16), hb, preferred_element_type=jnp.float32).astype(xs_ref.dtype)


def _route(h, w_router, b_router, layer):
    small = lambda dt: jax.ShapeDtypeStruct((T_ALL, TOP_K), dt)
    per_tile = jax.ShapeDtypeStruct((MOE_NT, 1, N_EXPERTS), jnp.int32)
    gates, ls, n_pad, off, xs = pl.pallas_call(
        _route_kernel,
        out_shape=[small(jnp.float32), small(jnp.int32), per_tile, per_tile,
                   jax.ShapeDtypeStruct((N_SLOTS, D_MODEL), jnp.bfloat16)],
        grid=(MOE_NT,),
        in_specs=[
            pl.BlockSpec((MOE_TT, D_MODEL), lambda i: (i, 0)),
            pl.BlockSpec((None, D_MODEL, N_EXPERTS), lambda i: (layer, 0, 0)),
            pl.BlockSpec((None, 1, N_EXPERTS), lambda i: (layer, 0, 0)),
        ],
        out_specs=[pl.BlockSpec((MOE_TT, TOP_K), lambda i: (i, 0))] * 2
        + [pl.BlockSpec((None, 1, N_EXPERTS), lambda i: (i, 0, 0))] * 2
        + [pl.BlockSpec((MOE_SMAX, D_MODEL), lambda i: (i, 0))],
        compiler_params=_cparams(("arbitrary",)),
        name="route",
    )(h, w_router, b_router.reshape(DEPTH, 1, N_EXPERTS))
    return gates, ls, n_pad.reshape(MOE_NT, N_EXPERTS), off.reshape(MOE_NT, N_EXPERTS), xs


def _ffn_kernel(sb_e, sb_i0, sb_i1, sb_nch, n_sb, g_src, g_n, g_dst,
                xs_ref, wg_ref, wu_ref, wd_ref, bg_ref, bu_ref, bd_ref,
                ys_ref, xbuf, yacc, wgb, wub, wdb, sem):
    del n_sb
    s = pl.program_id(0)
    j = pl.program_id(1)
    nf = pl.num_programs(1)
    nch = sb_nch[s]

    def for_pieces(fn):
        def body(i, carry):
            g = sb_e[s] * MOE_NT + i
            n = g_n[g]
            for size in DMA_SIZES:
                done = n & (-2 * size)

                @pl.when((n & size) != 0)
                def _(size=size, done=done):
                    fn(pl.multiple_of(g_src[g] + done, MOE_ALIGN),
                       pl.multiple_of(g_dst[g] + done, MOE_ALIGN), size)
            return carry
        lax.fori_loop(sb_i0[s], sb_i1[s], body, 0)

    def x_copy(hbm_row, vmem_row, size):
        return pltpu.make_async_copy(xs_ref.at[pl.ds(hbm_row, size)],
                                     xbuf.at[pl.ds(vmem_row, size)], sem.at[0])

    def y_copy(hbm_row, vmem_row, size):
        return pltpu.make_async_copy(xbuf.at[pl.ds(vmem_row, size)],
                                     ys_ref.at[pl.ds(hbm_row, size)], sem.at[1])

    def rows_at(row, size):
        return pl.ds(pl.multiple_of(row, MOE_HALF), size)

    def chunk_rows(c):
        return rows_at(c * MOE_CHUNK, MOE_CHUNK)

    def for_halves(fn):
        def body(c, carry):
            fn(rows_at(c * MOE_HALF, MOE_HALF))
            return carry
        lax.fori_loop(0, nch, body, 0)

    @pl.when((j == 0) & (nch > 0))
    def _():
        def clear(rows):
            xbuf[rows, :] = jnp.zeros((MOE_HALF, D_MODEL), xbuf.dtype)
        for_halves(clear)
        for_pieces(lambda *a: x_copy(*a).start())

        def init(rows):
            yacc[rows, :] = jnp.broadcast_to(bd_ref[...], (MOE_HALF, D_MODEL))
        for_halves(init)
        for_pieces(lambda *a: x_copy(*a).wait())

    @pl.when(nch > 0)
    def _():
        wgb[...] = wg_ref[...].astype(jnp.bfloat16)
        wub[...] = wu_ref[...].astype(jnp.bfloat16)
        wdb[...] = wd_ref[...].astype(jnp.bfloat16)

        def gate_up(rows):
            x = xbuf[rows, :]
            gate = jnp.dot(x, wgb[...], preferred_element_type=jnp.float32) + bg_ref[...]
            up = jnp.dot(x, wub[...], preferred_element_type=jnp.float32) + bu_ref[...]
            return gate, up

        def down(rows, gate, up):
            gate = jnp.minimum(gate, SWIGLU_LIMIT)
            up = jnp.clip(up, -SWIGLU_LIMIT, SWIGLU_LIMIT)
            act = (up + 1.0) * (gate / (1.0 + jnp.exp(-SWIGLU_ALPHA * gate)))
            yacc[rows, :] += jnp.dot(act.astype(jnp.bfloat16), wdb[...],
                                     preferred_element_type=jnp.float32)

        def run(row_slices):
            gu = gate_up(row_slices[0])
            for k, rows in enumerate(row_slices):
                nxt = gate_up(row_slices[k + 1]) if k + 1 < len(row_slices) else None
                down(rows, *gu)
                gu = nxt

        n_full = nch // 2

        def quad(qd, carry):
            run([chunk_rows(4 * qd + k) for k in range(4)])
            return carry
        lax.fori_loop(0, n_full // 4, quad, 0)

        @pl.when((n_full & 2) != 0)
        def _():
            base = n_full & -4
            run([chunk_rows(base), chunk_rows(base + 1)])

        @pl.when((n_full & 1) != 0)
        def _():
            run([chunk_rows(n_full - 1)])

        @pl.when((nch & 1) != 0)
        def _():
            run([rows_at(n_full * MOE_CHUNK, MOE_HALF)])

    @pl.when((j == nf - 1) & (nch > 0))
    def _():
        def stage(rows):
            xbuf[rows, :] = yacc[rows, :].astype(xbuf.dtype)
        for_halves(stage)
        for_pieces(lambda *a: y_copy(*a).start())
        for_pieces(lambda *a: y_copy(*a).wait())


N_FFN_PREFETCH = 8


def _expert_ffn(xs, tables, w_gu, b_gu, w_down, b_down, layer):
    nf = D_FF // MOE_TF

    def wspec(block, index):
        def index_map(s, j, sb_e, sb_i0, sb_i1, sb_nch, n_sb, *_):
            return index(sb_e[s], jnp.where(s < n_sb[0], j, nf - 1))
        return pl.BlockSpec(block, index_map)

    grid_spec = pltpu.PrefetchScalarGridSpec(
        num_scalar_prefetch=N_FFN_PREFETCH,
        grid=(N_SUPER, nf),
        in_specs=[
            pl.BlockSpec(memory_space=pl.ANY),
            wspec((None, None, D_MODEL, MOE_TF), lambda e, j: (layer, e, 0, j)),
            wspec((None, None, D_MODEL, MOE_TF), lambda e, j: (layer, e, 0, nf + j)),
            wspec((None, None, MOE_TF, D_MODEL), lambda e, j: (layer, e, j, 0)),
            wspec((None, None, 1, MOE_TF), lambda e, j: (layer, e, 0, j)),
            wspec((None, None, 1, MOE_TF), lambda e, j: (layer, e, 0, nf + j)),
            wspec((None, None, 1, D_MODEL), lambda e, j: (layer, e, 0, 0)),
        ],
        out_specs=pl.BlockSpec(memory_space=pl.ANY),
        scratch_shapes=[
            pltpu.VMEM((MOE_TMAX, D_MODEL), jnp.bfloat16),
            pltpu.VMEM((MOE_TMAX, D_MODEL), jnp.float32),
            pltpu.VMEM((D_MODEL, MOE_TF), jnp.bfloat16),
            pltpu.VMEM((D_MODEL, MOE_TF), jnp.bfloat16),
            pltpu.VMEM((MOE_TF, D_MODEL), jnp.bfloat16),
            pltpu.SemaphoreType.DMA((2,)),
        ],
    )
    b_gu4 = b_gu.reshape(DEPTH, N_EXPERTS, 1, 2 * D_FF)
    return pl.pallas_call(
        _ffn_kernel,
        out_shape=jax.ShapeDtypeStruct((N_SLOTS, D_MODEL), jnp.bfloat16),
        grid_spec=grid_spec,
        input_output_aliases={N_FFN_PREFETCH: 0},
        compiler_params=pltpu.CompilerParams(dimension_semantics=("arbitrary", "arbitrary"),
                                             vmem_limit_bytes=VMEM_LIMIT, has_side_effects=True),
        name="expert_ffn",
    )(*tables, xs, w_gu, w_gu, w_down, b_gu4, b_gu4, b_down.reshape(DEPTH, N_EXPERTS, 1, D_MODEL))


def _plan_superblocks(n_pad, off):
    i32 = jnp.int32
    cur = jnp.zeros((N_EXPERTS,), i32)
    n_closed = jnp.zeros((N_EXPERTS,), i32)
    local_sb, dst = [], []
    for i in range(MOE_NT):
        n = n_pad[i]
        overflow = cur + n > MOE_TMAX
        n_closed = n_closed + overflow.astype(i32)
        cur = jnp.where(overflow, n, cur + n)
        local_sb.append(n_closed)
        dst.append(cur - n)
    local_sb = jnp.stack(local_sb)
    dst = jnp.stack(dst)
    nsb_e = jnp.where(n_pad.sum(axis=0) > 0, n_closed + 1, 0)
    sb_end = jnp.cumsum(nsb_e)
    n_sb = sb_end[-1]
    group_sb = (sb_end - nsb_e)[None, :] + local_sb
    s = jnp.arange(N_SUPER, dtype=i32)
    e_of = jnp.minimum(jnp.searchsorted(sb_end, s, side='right'), N_EXPERTS - 1).astype(i32)
    live = s < n_sb
    mine = group_sb[:, e_of] == s[None, :]
    tile = jnp.arange(MOE_NT, dtype=i32)[:, None]
    i0 = jnp.min(jnp.where(mine, tile, MOE_NT), axis=0)
    i1 = jnp.max(jnp.where(mine, tile + 1, 0), axis=0)
    rows = jnp.sum(jnp.where(mine, n_pad[:, e_of], 0), axis=0)
    last_e = e_of[jnp.maximum(n_sb - 1, 0)]
    flat = lambda a: a.T.reshape(-1).astype(i32)
    return (jnp.where(live, e_of, last_e).astype(i32),
            jnp.where(live, i0, 0).astype(i32),
            jnp.where(live, i1, 0).astype(i32),
            jnp.where(live, (rows + MOE_HALF - 1) // MOE_HALF, 0).astype(i32),
            n_sb.reshape(1).astype(i32),
            flat(tile * MOE_SMAX + off), flat(n_pad), flat(dst))


def _combine_kernel(*refs, emit_h, gate_row, shift_row, scale_row):
    if emit_h:
        x_ref, ys_ref, ls_ref, g_ref, mod_ref, modn_ref, lng_ref, lnb_ref, xo_ref, h_ref = refs
    else:
        x_ref, ys_ref, ls_ref, g_ref, mod_ref, lng_ref, lnb_ref, xo_ref = refs
    ls = ls_ref[...]
    gts = g_ref[...]
    slot = lax.broadcasted_iota(jnp.int32, (ls.shape[0], MOE_SMAX), 1)
    weights = jnp.zeros(slot.shape, jnp.float32)
    for k in range(TOP_K):
        weights = jnp.where(slot == ls[:, k:k + 1], gts[:, k:k + 1], weights)
    sub = jnp.dot(weights.astype(jnp.bfloat16), ys_ref[...], preferred_element_type=jnp.float32)
    x = _post_norm(x_ref[...], sub, mod_ref[gate_row:gate_row + 1, :], lng_ref, lnb_ref)
    xo_ref[...] = x
    if emit_h:
        h_ref[...] = _modulate(x, modn_ref, shift_row, scale_row).astype(h_ref.dtype)


def _combine_norm(x, ys, ls, gates, ada4, layer, ln_g, ln_b, *, emit_h):
    tm = 256
    per = MOE_TT // tm
    row_spec = pl.BlockSpec((tm, D_MODEL), lambda i, r: (i * per + r, 0))
    small_spec = pl.BlockSpec((tm, TOP_K), lambda i, r: (i * per + r, 0))
    in_specs = [row_spec, pl.BlockSpec((MOE_SMAX, D_MODEL), lambda i, r: (i, 0)), small_spec, small_spec,
                _ada_spec(layer, tm, per)]
    args = [x, ys, ls, gates, ada4]
    out_shape = [jax.ShapeDtypeStruct((T_ALL, D_MODEL), jnp.float32)]
    out_specs = [row_spec]
    if emit_h:
        in_specs.append(_ada_spec(layer + 1, tm, per))
        args.append(ada4)
        out_shape.append(jax.ShapeDtypeStruct((T_ALL, D_MODEL), jnp.bfloat16))
        out_specs.append(row_spec)
    ln_specs, ln_arrs = _ln_args(ln_g, ln_b, layer, 1)
    kern = functools.partial(_combine_kernel, emit_h=emit_h, gate_row=5, shift_row=0, scale_row=1)
    return pl.pallas_call(
        kern, out_shape=out_shape, grid=(MOE_NT, per), in_specs=in_specs + ln_specs,
        out_specs=out_specs, compiler_params=_cparams(("arbitrary", "arbitrary")), name="combine",
    )(*args, *ln_arrs)


def kernel(x_prompt, x_sample, cache_diff_k, cache_diff_v, cache_win_k, cache_win_v, c, c_ctx, w_ada, b_ada, ln_g, ln_b, diff_wq, diff_wk, diff_wv, diff_wo, diff_lambda, diff_subln_g, win_wq, win_wk, win_wv, win_wo, win_sink, pool_w, pool_b, pool_scale, moe_w_router, moe_b_router, moe_w_gu, moe_b_gu, moe_w_down, moe_b_down):
    f32, bf16 = jnp.float32, jnp.bfloat16
    x = jnp.concatenate([x_prompt.reshape(T_P, D_MODEL), x_sample.reshape(T_S, D_MODEL)], axis=0)
    cvec = jnp.concatenate([c_ctx[None, :], c, jnp.zeros((COND_ROWS - N_COND, D_MODEL), f32)], axis=0)
    ada4 = _ada_all(cvec, w_ada, b_ada).reshape(DEPTH, COND_ROWS, N_ADA, D_MODEL)
    rope_a = _rope_tables(DK_A)
    rope_b = _rope_tables(DH_B)
    n_la = cache_diff_k.shape[1]
    n_lb = cache_win_k.shape[1]
    cdk = cache_diff_k.reshape(DEC_BATCH, n_la, PAST_LEN, D_MODEL)
    cdv = cache_diff_v.reshape(DEC_BATCH, n_la, PAST_LEN, D_MODEL)
    cwk = cache_win_k.reshape(DEC_BATCH, n_lb, PAST_LEN, KV_B * DH_B)
    cwv = cache_win_v.reshape(DEC_BATCH, n_lb, PAST_LEN, KV_B * DH_B)

    (h,) = _modnorm(x, None, ada4, 0, ln_g, ln_b, h_dtype=bf16, shift_row=0, scale_row=1)
    st_dk, st_dv, st_wk, st_wv = [], [], [], []
    ia = ib = ic = 0
    for i in range(DEPTH):
        kind = i % N_MIXERS
        if kind == 0:
            lam_init = 0.8 - 0.6 * math.exp(-0.3 * i)
            pj = functools.partial(_matmul, h, layer=ia)
            q_p = pj(diff_wq, row_start=0, n_rows=T_P, out_dtype=bf16)
            k_p = pj(diff_wk, row_start=0, n_rows=T_P, out_dtype=f32)
            v_p = pj(diff_wv, row_start=0, n_rows=T_P, out_dtype=f32)
            q_s = pj(diff_wq, row_start=T_P, n_rows=T_S, out_dtype=bf16, rope=rope_a)
            k_s = pj(diff_wk, row_start=T_P, n_rows=T_S, out_dtype=bf16, rope=rope_a)
            v_s = pj(diff_wv, row_start=T_P, n_rows=T_S, out_dtype=bf16)
            seq_spec = pl.BlockSpec((SEQ, DV_A), lambda b, hh, qi: (b, hh))
            o = _diff_attention(q_p, [(k_p, seq_spec)], [(v_p, seq_spec)], diff_lambda, diff_subln_g,
                                ia, lam_init, n_batch=BATCH, n_q=SEQ, tq=SEQ, out_row_start=0,
                                prev_out=None)
            cache_spec = pl.BlockSpec((None, None, PAST_LEN, DV_A), lambda b, hh, qi, ia=ia: (b, ia, 0, hh))
            lat_spec = pl.BlockSpec((DEC_SEQ, DV_A), lambda b, hh, qi: (b, hh))
            o = _diff_attention(q_s, [(cdk, cache_spec), (k_s, lat_spec)],
                                [(cdv, cache_spec), (v_s, lat_spec)], diff_lambda, diff_subln_g,
                                ia, lam_init, n_batch=DEC_BATCH, n_q=DEC_SEQ, tq=256,
                                out_row_start=T_P, prev_out=o)
            sub = _matmul(o, diff_wo, ia, row_start=0, n_rows=T_ALL, out_dtype=f32)
            st_dk.append(k_p.reshape(BATCH, SEQ, H_A, 2, DK_A))
            st_dv.append(v_p.reshape(BATCH, SEQ, H_A, DV_A))
            ia += 1
        elif kind == 1:
            pj = functools.partial(_matmul, h, layer=ib)
            q_p = pj(win_wq, row_start=0, n_rows=T_P, out_dtype=bf16)
            k_p = pj(win_wk, row_start=0, n_rows=T_P, out_dtype=f32)
            v_p = pj(win_wv, row_start=0, n_rows=T_P, out_dtype=f32)
            q_s = pj(win_wq, row_start=T_P, n_rows=T_S, out_dtype=bf16, rope=rope_b)
            k_s = pj(win_wk, row_start=T_P, n_rows=T_S, out_dtype=bf16, rope=rope_b)
            v_s = pj(win_wv, row_start=T_P, n_rows=T_S, out_dtype=bf16)
            kvw = KV_B * DH_B
            seq_spec = pl.BlockSpec((SEQ, kvw), lambda b, qi: (b, 0))
            o = _gqa_attention(q_p, [(k_p, seq_spec)], [(v_p, seq_spec)], win_sink, ib,
                               n_batch=BATCH, n_q=SEQ, tq=SEQ, band=0, out_row_start=0, prev_out=None)
            cache_spec = pl.BlockSpec((None, None, PAST_LEN, kvw), lambda b, qi, ib=ib: (b, ib, 0, 0))
            lat_spec = pl.BlockSpec((DEC_SEQ, kvw), lambda b, qi: (b, 0))
            o = _gqa_attention(q_s, [(cwk, cache_spec), (k_s, lat_spec)],
                               [(cwv, cache_spec), (v_s, lat_spec)], win_sink, ib,
                               n_batch=DEC_BATCH, n_q=DEC_SEQ, tq=WINDOW, band=3 * WINDOW,
                               out_row_start=T_P, prev_out=o)
            sub = _matmul(o, win_wo, ib, row_start=0, n_rows=T_ALL, out_dtype=f32)
            st_wk.append(k_p.reshape(BATCH, SEQ, KV_B, DH_B))
            st_wv.append(v_p.reshape(BATCH, SEQ, KV_B, DH_B))
            ib += 1
        else:
            sub = _pool_mixer(h, pool_w, pool_b, pool_scale, ic)
            ic += 1
        x, h = _modnorm(x, sub, ada4, i, ln_g, ln_b, h_dtype=f32, gate_row=2, shift_row=3, scale_row=4)
        gates, ls, n_pad, off, xs = _route(h, moe_w_router, moe_b_router, i)
        ys = _expert_ffn(xs, _plan_superblocks(n_pad, off), moe_w_gu, moe_b_gu, moe_w_down,
                         moe_b_down, i)
        outs = _combine_norm(x, ys, ls, gates, ada4, i, ln_g, ln_b, emit_h=i < DEPTH - 1)
        x = outs[0]
        if i < DEPTH - 1:
            h = outs[1]
    y_prompt = x[:T_P].reshape(BATCH, SEQ, D_MODEL)
    y_sample = x[T_P:].reshape(DEC_BATCH, DEC_SEQ, D_MODEL)
    return (y_prompt, y_sample, jnp.stack(st_dk, axis=1), jnp.stack(st_dv, axis=1),
            jnp.stack(st_wk, axis=1), jnp.stack(st_wv, axis=1))
```

```python
import functools
import math

import jax
import jax.numpy as jnp
from jax import lax
from jax.experimental import pallas as pl
from jax.experimental.pallas import tpu as pltpu

D_MODEL = 2048
BATCH = 16
SEQ = 256
DEPTH = 4
DEC_BATCH = 2
DEC_SEQ = 2048
PAST_LEN = 512
GRID_W = 64
N_MIXERS = 3
H_A = 8
DK_A = D_MODEL // (2 * H_A)
DV_A = 2 * DK_A
DH_B = 64
H_B = D_MODEL // DH_B
KV_B = 4
G_B = H_B // KV_B
WINDOW = 128
POOL_SIZES = (2, 4, 8, 16)
POOL_GROUP = D_MODEL // len(POOL_SIZES)
N_EXPERTS = 32
TOP_K = 4
D_FF = D_MODEL
SWIGLU_ALPHA = 1.702
SWIGLU_LIMIT = 7.0
ROPE_BASE = 10000.0
LN_EPS = 1e-5
DEEPNORM_ALPHA = (2 * DEPTH) ** 0.25
LOG2_E = math.log2(math.e)
N_ADA = 6

T_P = BATCH * SEQ
T_S = DEC_BATCH * DEC_SEQ
T_ALL = T_P + T_S
N_COND = 1 + DEC_BATCH
COND_ROWS = 8

LANES = 128
VMEM_LIMIT = 52 * 1024 * 1024
BF16_ROWS = 16
MOE_TT = 512
MOE_NT = T_ALL // MOE_TT
MOE_ALIGN = BF16_ROWS
MOE_SMAX = MOE_TT * TOP_K + N_EXPERTS * MOE_ALIGN
N_SLOTS = MOE_NT * MOE_SMAX
MOE_CHUNK = 256
MOE_HALF = MOE_CHUNK // 2
MOE_TMAX = 2048
MOE_TF = 256
N_SUPER = N_EXPERTS + N_SLOTS // (MOE_TMAX - MOE_TT)
DMA_SIZES = (512, 256, 128, 64, 32, 16)


def _cparams(sem, vmem=VMEM_LIMIT):
    return pltpu.CompilerParams(dimension_semantics=sem, vmem_limit_bytes=vmem)


def _cond_group(row):
    return jnp.maximum(row // DEC_SEQ - (T_P // DEC_SEQ - 1), 0)


def _ada_kernel(c_ref, w_ref, b_ref, o_ref):
    cv = c_ref[...]
    s = (cv / (1.0 + jnp.exp(-cv))).astype(jnp.bfloat16)
    o_ref[...] = jnp.dot(s, w_ref[...].astype(jnp.bfloat16),
                         preferred_element_type=jnp.float32) + b_ref[...]


def _ada_all(cvec, w_ada, b_ada):
    tn = 1024
    n = N_ADA * D_MODEL
    return pl.pallas_call(
        _ada_kernel,
        out_shape=jax.ShapeDtypeStruct((DEPTH, COND_ROWS, n), jnp.float32),
        grid=(DEPTH, n // tn),
        in_specs=[
            pl.BlockSpec((COND_ROWS, D_MODEL), lambda l, j: (0, 0)),
            pl.BlockSpec((None, D_MODEL, tn), lambda l, j: (l, 0, j)),
            pl.BlockSpec((None, 1, tn), lambda l, j: (l, 0, j)),
        ],
        out_specs=pl.BlockSpec((None, COND_ROWS, tn), lambda l, j: (l, 0, j)),
        compiler_params=_cparams(("arbitrary", "arbitrary")),
        name="ada",
    )(cvec, w_ada, b_ada.reshape(DEPTH, 1, n))


def _post_norm(x, sub, gate, lng_ref, lnb_ref):
    xf = DEEPNORM_ALPHA * x + gate * sub
    mu = jnp.mean(xf, axis=-1, keepdims=True)
    xc = xf - mu
    var = jnp.mean(xc * xc, axis=-1, keepdims=True)
    return xc * lax.rsqrt(var + LN_EPS) * lng_ref[...] + lnb_ref[...]


def _modulate(x, mod_ref, shift_row, scale_row):
    return x * (1.0 + mod_ref[scale_row:scale_row + 1, :]) + mod_ref[shift_row:shift_row + 1, :]


def _modnorm_kernel(*refs, do_norm, split_tiles, gate_row, shift_row, scale_row):
    if split_tiles:
        xp_ref, xs_ref, *refs = refs
        x = jnp.where(pl.program_id(0) < split_tiles, xp_ref[...], xs_ref[...])
    else:
        x_ref, *refs = refs
        x = x_ref[...]
    if do_norm:
        sub_ref, mod_ref, lng_ref, lnb_ref, xo_ref, h_ref = refs
        x = _post_norm(x, sub_ref[...], mod_ref[gate_row:gate_row + 1, :], lng_ref, lnb_ref)
        xo_ref[...] = x
    else:
        mod_ref, h_ref = refs
    h_ref[...] = _modulate(x, mod_ref, shift_row, scale_row).astype(h_ref.dtype)


def _ada_spec(layer, tm, rows_per_step=1, step_start=0):
    return pl.BlockSpec(
        (None, None, N_ADA, D_MODEL),
        lambda i, *_: (layer, _cond_group((i + step_start) * tm * rows_per_step), 0, 0))


def _ln_args(ln_g, ln_b, layer, which):
    spec = pl.BlockSpec((None, None, 1, D_MODEL), lambda i, *_: (layer, which, 0, 0))
    return [spec, spec], [ln_g.reshape(DEPTH, 2, 1, D_MODEL), ln_b.reshape(DEPTH, 2, 1, D_MODEL)]


def _modnorm(x, sub, ada4, layer, ln_g, ln_b, *, h_dtype, gate_row=0, shift_row=0, scale_row=0):
    tm = 256
    do_norm = sub is not None
    row_spec = pl.BlockSpec((tm, D_MODEL), lambda i: (i, 0))
    split_tiles = 0
    if isinstance(x, tuple):
        split_tiles = T_P // tm
        in_specs = [pl.BlockSpec((tm, D_MODEL), lambda i: (jnp.minimum(i, split_tiles - 1), 0)),
                    pl.BlockSpec((tm, D_MODEL), lambda i: (jnp.maximum(i - split_tiles, 0), 0))]
        args = list(x)
    else:
        in_specs, args = [row_spec], [x]
    if do_norm:
        in_specs.append(row_spec)
        args.append(sub)
    in_specs.append(_ada_spec(layer, tm))
    args.append(ada4)
    out_shape = [jax.ShapeDtypeStruct((T_ALL, D_MODEL), h_dtype)]
    out_specs = [row_spec]
    if do_norm:
        ln_specs, ln_arrs = _ln_args(ln_g, ln_b, layer, 0)
        in_specs += ln_specs
        args += ln_arrs
        out_shape.insert(0, jax.ShapeDtypeStruct((T_ALL, D_MODEL), jnp.float32))
        out_specs.insert(0, row_spec)
    kern = functools.partial(_modnorm_kernel, do_norm=do_norm, split_tiles=split_tiles,
                             gate_row=gate_row, shift_row=shift_row, scale_row=scale_row)
    return pl.pallas_call(
        kern, out_shape=out_shape, grid=(T_ALL // tm,), in_specs=in_specs, out_specs=out_specs,
        compiler_params=_cparams(("arbitrary",)), name="modnorm",
    )(*args)


def _mm_kernel(*refs, rope_quarter, has_rope):
    if has_rope:
        a_ref, w_ref, cos_ref, sin_ref, o_ref, wb_ref = refs
    else:
        a_ref, w_ref, o_ref, wb_ref = refs

    @pl.when(pl.program_id(1) == 0)
    def _():
        wb_ref[...] = w_ref[...].astype(jnp.bfloat16)

    acc = jnp.dot(a_ref[...], wb_ref[...], preferred_element_type=jnp.float32)
    if has_rope:
        q = rope_quarter
        cs = cos_ref[...]
        sn = sin_ref[...]
        lane = lax.broadcasted_iota(jnp.int32, (acc.shape[0], LANES), 1)
        first = (lane % (2 * q)) < q
        for c in range(acc.shape[1] // LANES):
            a = acc[:, c * LANES:(c + 1) * LANES]
            partner = jnp.where(first, pltpu.roll(a, LANES - q, 1), pltpu.roll(a, q, 1))
            o_ref[:, c * LANES:(c + 1) * LANES] = (a * cs + partner * sn).astype(o_ref.dtype)
    else:
        o_ref[...] = acc.astype(o_ref.dtype)


def _matmul(a, w, layer, *, row_start, n_rows, out_dtype, rope=None, tm=1024, tn=512):
    k = a.shape[1]
    n = w.shape[2]
    tn = min(tn, n)
    m_off = row_start // tm
    in_specs = [
        pl.BlockSpec((tm, k), lambda j, i: (i + m_off, 0)),
        pl.BlockSpec((None, k, tn), lambda j, i: (layer, 0, j)),
    ]
    args = [a, w]
    has_rope = rope is not None
    quarter = 0
    if has_rope:
        cos_t, sin_t, quarter = rope
        per = DEC_SEQ // tm
        in_specs += [pl.BlockSpec((tm, LANES), lambda j, i: (i % per, 0))] * 2
        args += [cos_t, sin_t]
    return pl.pallas_call(
        functools.partial(_mm_kernel, rope_quarter=quarter, has_rope=has_rope),
        out_shape=jax.ShapeDtypeStruct((n_rows, n), out_dtype),
        grid=(n // tn, n_rows // tm),
        in_specs=in_specs,
        out_specs=pl.BlockSpec((tm, tn), lambda j, i: (i, j)),
        scratch_shapes=[pltpu.VMEM((k, tn), jnp.bfloat16)],
        compiler_params=_cparams(("arbitrary", "arbitrary")),
        name="proj",
    )(*args)


def _rope_tables(head_dim):
    quarter = head_dim // 4
    pos = jnp.arange(DEC_SEQ)
    row = (pos // GRID_W).astype(jnp.float32)
    col = (pos % GRID_W).astype(jnp.float32)
    inv = ROPE_BASE ** (-jnp.arange(quarter, dtype=jnp.float32) / quarter)
    ang_r = row[:, None] * inv
    ang_c = col[:, None] * inv
    ang = jnp.concatenate([ang_r, ang_r, ang_c, ang_c], axis=1)
    sign = jnp.concatenate([-jnp.ones(quarter), jnp.ones(quarter)] * 2)
    reps = LANES // head_dim
    cos_t = jnp.tile(jnp.cos(ang), (1, reps))
    sin_t = jnp.tile(jnp.sin(ang) * sign[None, :], (1, reps))
    return cos_t.astype(jnp.float32), sin_t.astype(jnp.float32), quarter


def _diff_attn_kernel(*refs, n_parts, lam_init):
    q_ref = refs[0]
    k_refs = refs[1:1 + n_parts]
    v_refs = refs[1 + n_parts:1 + 2 * n_parts]
    lam_ref, g_ref, o_ref = refs[1 + 2 * n_parts:]
    scale = DK_A ** -0.5
    lp = lam_ref[...]
    lam = (jnp.exp(jnp.sum(lp[0:1] * lp[1:2], axis=-1, keepdims=True))
           - jnp.exp(jnp.sum(lp[2:3] * lp[3:4], axis=-1, keepdims=True)) + lam_init)
    q = q_ref[...]
    v_vals = [vr[...].astype(jnp.bfloat16) for vr in v_refs]
    outs = []
    for c in range(2):
        qc = q[:, c * DK_A:(c + 1) * DK_A]
        s_parts = []
        for kr in k_refs:
            kc = kr[:, c * DK_A:(c + 1) * DK_A].astype(jnp.bfloat16)
            s_parts.append(lax.dot_general(qc, kc, (((1,), (1,)), ((), ())),
                                           preferred_element_type=jnp.float32))
        m = s_parts[0].max(axis=-1, keepdims=True)
        for s in s_parts[1:]:
            m = jnp.maximum(m, s.max(axis=-1, keepdims=True))
        e_parts = [jnp.exp2((s - m) * (scale * LOG2_E)) for s in s_parts]
        den = e_parts[0].sum(axis=-1, keepdims=True)
        for e in e_parts[1:]:
            den = den + e.sum(axis=-1, keepdims=True)
        oc = None
        for e, v in zip(e_parts, v_vals):
            contrib = jnp.dot(e.astype(jnp.bfloat16), v, preferred_element_type=jnp.float32)
            oc = contrib if oc is None else oc + contrib
        outs.append(oc / den)
    o = outs[0] - lam * outs[1]
    ms = jnp.mean(o * o, axis=-1, keepdims=True)
    o = o * lax.rsqrt(ms + LN_EPS) * g_ref[...] * (1.0 - lam_init)
    o_ref[...] = o.astype(o_ref.dtype)


def _diff_attention(q, k_parts, v_parts, lam_p, subln_g, ia, lam_init, *, n_batch, n_q, tq,
                    out_row_start, prev_out):
    nqb = n_q // tq
    row_off = out_row_start // tq
    in_specs = [pl.BlockSpec((tq, DV_A), lambda b, h, i: (b * nqb + i, h))]
    args = [q]
    for arr, spec in k_parts + v_parts:
        in_specs.append(spec)
        args.append(arr)
    in_specs.append(pl.BlockSpec((None, 4, DK_A), lambda b, h, i: (ia, 0, 0)))
    in_specs.append(pl.BlockSpec((None, 1, DV_A), lambda b, h, i: (ia, 0, 0)))
    args += [lam_p, subln_g.reshape(-1, 1, DV_A)]
    aliases = {}
    if prev_out is not None:
        in_specs.append(pl.BlockSpec(memory_space=pl.ANY))
        args.append(prev_out)
        aliases = {len(args) - 1: 0}

    def kern(*refs):
        if prev_out is not None:
            refs = refs[:-2] + refs[-1:]
        _diff_attn_kernel(*refs, n_parts=len(k_parts), lam_init=lam_init)

    return pl.pallas_call(
        kern,
        out_shape=jax.ShapeDtypeStruct((T_ALL, D_MODEL), jnp.bfloat16),
        grid=(n_batch, H_A, nqb),
        in_specs=in_specs,
        out_specs=pl.BlockSpec((tq, DV_A), lambda b, h, i: (row_off + b * nqb + i, h)),
        input_output_aliases=aliases,
        compiler_params=_cparams(("arbitrary", "arbitrary", "arbitrary")),
        name="diff_attn",
    )(*args)


def _gqa_kernel(*refs, n_parts, band, tq, n_lat):
    q_ref = refs[0]
    k_refs = refs[1:1 + n_parts]
    v_refs = refs[1 + n_parts:1 + 2 * n_parts]
    sink_ref, o_ref = refs[1 + 2 * n_parts:]
    scale = DH_B ** -0.5
    i = pl.program_id(1)
    q = q_ref[...]
    sink = sink_ref[...]
    grp = lax.broadcasted_iota(jnp.int32, (G_B * tq, 1), 0) // tq
    if band:
        start = pl.multiple_of(jnp.clip(i * tq - WINDOW, 0, n_lat - band), WINDOW)
        qpos = i * tq + (lax.broadcasted_iota(jnp.int32, (G_B * tq, band), 0) & (tq - 1))
        kpos = start + lax.broadcasted_iota(jnp.int32, (G_B * tq, band), 1)
        valid = jnp.abs(qpos - kpos) <= WINDOW
    k_vals, v_vals = [], []
    for p in range(n_parts):
        if band and p == n_parts - 1:
            k_vals.append(k_refs[p][pl.ds(start, band), :].astype(jnp.bfloat16))
            v_vals.append(v_refs[p][pl.ds(start, band), :].astype(jnp.bfloat16))
        else:
            k_vals.append(k_refs[p][...].astype(jnp.bfloat16))
            v_vals.append(v_refs[p][...].astype(jnp.bfloat16))
    for kk in range(KV_B):
        heads = [kk * G_B + g for g in range(G_B)]
        qs = jnp.concatenate([q[:, h * DH_B:(h + 1) * DH_B] for h in heads], axis=0)
        sk = jnp.zeros((G_B * tq, 1), jnp.float32)
        for g, h in enumerate(heads):
            sk = jnp.where(grp == g, sink[:, h:h + 1], sk)
        s_parts = []
        for p in range(n_parts):
            kc = k_vals[p][:, kk * DH_B:(kk + 1) * DH_B]
            s = lax.dot_general(qs, kc, (((1,), (1,)), ((), ())),
                                preferred_element_type=jnp.float32)
            if band and p == n_parts - 1:
                s = jnp.where(valid, s, -1e30)
            s_parts.append(s)
        m = sk * (1.0 / scale)
        sk_raw = m
        for s in s_parts:
            m = jnp.maximum(m, s.max(axis=-1, keepdims=True))
        e_parts = [jnp.exp2((s - m) * (scale * LOG2_E)) for s in s_parts]
        den = jnp.exp2((sk_raw - m) * (scale * LOG2_E))
        for e in e_parts:
            den = den + e.sum(axis=-1, keepdims=True)
        o = None
        for p in range(n_parts):
            contrib = jnp.dot(e_parts[p].astype(jnp.bfloat16), v_vals[p][:, kk * DH_B:(kk + 1) * DH_B],
                              preferred_element_type=jnp.float32)
            o = contrib if o is None else o + contrib
        o = o / den
        o_ref[:, kk * G_B * DH_B:(kk + 1) * G_B * DH_B] = jnp.concatenate(
            [o[g * tq:(g + 1) * tq] for g in range(G_B)], axis=1).astype(o_ref.dtype)


def _gqa_attention(q, k_parts, v_parts, sink, ib, *, n_batch, n_q, tq, band, out_row_start, prev_out):
    nqb = n_q // tq
    row_off = out_row_start // tq
    in_specs = [pl.BlockSpec((tq, D_MODEL), lambda b, i: (b * nqb + i, 0))]
    args = [q]
    for arr, spec in k_parts + v_parts:
        in_specs.append(spec)
        args.append(arr)
    in_specs.append(pl.BlockSpec((None, 1, H_B), lambda b, i: (ib, 0, 0)))
    args.append(sink.reshape(-1, 1, H_B))
    aliases = {}
    if prev_out is not None:
        in_specs.append(pl.BlockSpec(memory_space=pl.ANY))
        args.append(prev_out)
        aliases = {len(args) - 1: 0}

    def kern(*refs):
        if prev_out is not None:
            refs = refs[:-2] + refs[-1:]
        _gqa_kernel(*refs, n_parts=len(k_parts), band=band, tq=tq, n_lat=n_q)

    return pl.pallas_call(
        kern,
        out_shape=jax.ShapeDtypeStruct((T_ALL, D_MODEL), jnp.bfloat16),
        grid=(n_batch, nqb),
        in_specs=in_specs,
        out_specs=pl.BlockSpec((tq, D_MODEL), lambda b, i: (row_off + b * nqb + i, 0)),
        input_output_aliases=aliases,
        compiler_params=_cparams(("arbitrary", "arbitrary")),
        name="gqa_attn",
    )(*args)


def _pool_kernel(h_ref, w_ref, b_ref, ls_ref, o_ref, *, rows):
    i = pl.program_id(0)
    g = pl.program_id(1)
    seq = jnp.where(i * rows < T_P, SEQ, DEC_SEQ)
    pos = lax.broadcasted_iota(jnp.int32, (rows, 1), 0) & (seq - 1)

    def shifted(a, k):
        ok = (pos + k >= 0) & (pos + k < seq)
        return jnp.where(ok, pltpu.roll(a, (-k) % rows, 0), 0.0)

    for gi, wsz in enumerate(POOL_SIZES):
        @pl.when(g == gi)
        def _(wsz=wsz):
            h = h_ref[...].astype(jnp.float32)
            half = wsz // 2
            fwd = h
            m = 1
            while m < half:
                fwd = fwd + shifted(fwd, m)
                m *= 2
            bwd = shifted(h, -1)
            m = 1
            while m < half:
                bwd = bwd + shifted(bwd, -m)
                m *= 2
            lo = jnp.maximum(pos - half, 0)
            hi = jnp.minimum(pos + half - 1, seq - 1)
            cnt = (hi - lo + 1).astype(jnp.float32)
            diff = ((fwd + bwd) / cnt - h).astype(jnp.bfloat16)
            y = jnp.dot(diff, w_ref[...].astype(jnp.bfloat16), preferred_element_type=jnp.float32)
            o_ref[...] = (y + b_ref[...]) * ls_ref[...]


def _pool_mixer(h, pool_w, pool_b, pool_scale, ic):
    rows = DEC_SEQ
    ng = len(POOL_SIZES)
    return pl.pallas_call(
        functools.partial(_pool_kernel, rows=rows),
        out_shape=jax.ShapeDtypeStruct((T_ALL, D_MODEL), jnp.float32),
        grid=(T_ALL // rows, ng),
        in_specs=[
            pl.BlockSpec((rows, POOL_GROUP), lambda i, g: (i, g)),
            pl.BlockSpec((None, None, POOL_GROUP, POOL_GROUP), lambda i, g: (ic, g, 0, 0)),
            pl.BlockSpec((None, None, 1, POOL_GROUP), lambda i, g: (ic, g, 0, 0)),
            pl.BlockSpec((None, None, 1, POOL_GROUP), lambda i, g: (ic, g, 0, 0)),
        ],
        out_specs=pl.BlockSpec((rows, POOL_GROUP), lambda i, g: (i, g)),
        compiler_params=_cparams(("arbitrary", "arbitrary")),
        name="pool",
    )(h, pool_w, pool_b.reshape(-1, ng, 1, POOL_GROUP), pool_scale.reshape(-1, ng, 1, POOL_GROUP))


def _route_kernel(h_ref, w_ref, b_ref, g_ref, ls_ref, n_ref, off_ref, xs_ref):
    tt = h_ref.shape[0]
    h = h_ref[...]
    logits = jnp.dot(h, w_ref[...], precision=lax.Precision.HIGHEST,
                     preferred_element_type=jnp.float32) + b_ref[...]
    lane = lax.broadcasted_iota(jnp.int32, logits.shape, 1).astype(jnp.float32)
    work = logits
    vals, idxs = [], []
    member = jnp.zeros(logits.shape, jnp.float32)
    for _ in range(TOP_K):
        m = work.max(axis=-1, keepdims=True)
        idx = jnp.where(work == m, lane, float(N_EXPERTS)).min(axis=-1, keepdims=True)
        hit = lane == idx
        member = jnp.where(hit, 1.0, member)
        work = jnp.where(hit, -jnp.inf, work)
        vals.append(m)
        idxs.append(idx)
    es = [jnp.exp(v - vals[0]) for v in vals]
    den = es[0]
    for e in es[1:]:
        den = den + e
    r_i = lax.broadcasted_iota(jnp.int32, (tt, tt), 0)
    c_i = lax.broadcasted_iota(jnp.int32, (tt, tt), 1)
    tri = jnp.where(c_i < r_i, 1.0, 0.0).astype(jnp.bfloat16)
    before = jnp.dot(tri, member.astype(jnp.bfloat16), preferred_element_type=jnp.float32)
    n_units = jnp.floor((member.sum(axis=0, keepdims=True) + (MOE_ALIGN - 1)) * (1.0 / MOE_ALIGN))
    e_r = lax.broadcasted_iota(jnp.int32, (N_EXPERTS, N_EXPERTS), 0)
    e_c = lax.broadcasted_iota(jnp.int32, (N_EXPERTS, N_EXPERTS), 1)
    upper = jnp.where(e_r < e_c, 1.0, 0.0).astype(jnp.bfloat16)
    off_units = jnp.dot(jnp.broadcast_to(n_units, (8, N_EXPERTS)).astype(jnp.bfloat16), upper,
                        preferred_element_type=jnp.float32)[0:1]
    n_ref[...] = (n_units * MOE_ALIGN).astype(jnp.int32)
    off_ref[...] = (off_units * MOE_ALIGN).astype(jnp.int32)
    slot_of = off_units * MOE_ALIGN + before
    lane128 = lax.broadcasted_iota(jnp.int32, (tt, LANES), 1)
    ls_mat = jnp.full((tt, LANES), -1.0, jnp.float32)
    for k in range(TOP_K):
        ls_k = jnp.where(lane == idxs[k], slot_of, 0.0).sum(axis=-1, keepdims=True)
        g_ref[:, k:k + 1] = es[k] / den
        ls_ref[:, k:k + 1] = ls_k.astype(jnp.int32)
        ls_mat = jnp.where(lane128 == k, ls_k, ls_mat)
    ls_t = ls_mat.T
    hb = h.astype(jnp.bfloat16)
    for c in range(MOE_SMAX // tt):
        srow = (lax.broadcasted_iota(jnp.int32, (tt, tt), 0) + c * tt).astype(jnp.float32)
        sel = jnp.zeros((tt, tt), jnp.float32)
        for k in range(TOP_K):
            sel = jnp.where(srow == ls_t[k:k + 1, :], 1.0, sel)
        xs_ref[c * tt:(c + 1) * tt, :] = jnp.dot(
            sel.astype(jnp.bfloat16), hb, preferred_element_type=jnp.float32).astype(xs_ref.dtype)


def _route(h, w_router, b_router, layer):
    small = lambda dt: jax.ShapeDtypeStruct((T_ALL, TOP_K), dt)
    per_tile = jax.ShapeDtypeStruct((MOE_NT, 1, N_EXPERTS), jnp.int32)
    gates, ls, n_pad, off, xs = pl.pallas_call(
        _route_kernel,
        out_shape=[small(jnp.float32), small(jnp.int32), per_tile, per_tile,
                   jax.ShapeDtypeStruct((N_SLOTS, D_MODEL), jnp.bfloat16)],
        grid=(MOE_NT,),
        in_specs=[
            pl.BlockSpec((MOE_TT, D_MODEL), lambda i: (i, 0)),
            pl.BlockSpec((None, D_MODEL, N_EXPERTS), lambda i: (layer, 0, 0)),
            pl.BlockSpec((None, 1, N_EXPERTS), lambda i: (layer, 0, 0)),
        ],
        out_specs=[pl.BlockSpec((MOE_TT, TOP_K), lambda i: (i, 0))] * 2
        + [pl.BlockSpec((None, 1, N_EXPERTS), lambda i: (i, 0, 0))] * 2
        + [pl.BlockSpec((MOE_SMAX, D_MODEL), lambda i: (i, 0))],
        compiler_params=_cparams(("arbitrary",)),
        name="route",
    )(h, w_router, b_router.reshape(DEPTH, 1, N_EXPERTS))
    return gates, ls, n_pad.reshape(MOE_NT, N_EXPERTS), off.reshape(MOE_NT, N_EXPERTS), xs


def _ffn_kernel(sb_e, sb_i0, sb_i1, sb_nch, n_sb, g_src, g_n, g_dst,
                xs_ref, wg_ref, wu_ref, wd_ref, bg_ref, bu_ref, bd_ref,
                ys_ref, xbuf, yacc, wgb, wub, wdb, sem):
    del n_sb
    s = pl.program_id(0)
    j = pl.program_id(1)
    nf = pl.num_programs(1)
    nch = sb_nch[s]

    def for_pieces(fn):
        def body(i, carry):
            g = sb_e[s] * MOE_NT + i
            n = g_n[g]
            for size in DMA_SIZES:
                done = n & (-2 * size)

                @pl.when((n & size) != 0)
                def _(size=size, done=done):
                    fn(pl.multiple_of(g_src[g] + done, MOE_ALIGN),
                       pl.multiple_of(g_dst[g] + done, MOE_ALIGN), size)
            return carry
        lax.fori_loop(sb_i0[s], sb_i1[s], body, 0)

    def x_copy(hbm_row, vmem_row, size):
        return pltpu.make_async_copy(xs_ref.at[pl.ds(hbm_row, size)],
                                     xbuf.at[pl.ds(vmem_row, size)], sem.at[0])

    def y_copy(hbm_row, vmem_row, size):
        return pltpu.make_async_copy(xbuf.at[pl.ds(vmem_row, size)],
                                     ys_ref.at[pl.ds(hbm_row, size)], sem.at[1])

    def rows_at(row, size):
        return pl.ds(pl.multiple_of(row, MOE_HALF), size)

    def chunk_rows(c):
        return rows_at(c * MOE_CHUNK, MOE_CHUNK)

    def for_halves(fn):
        def body(c, carry):
            fn(rows_at(c * MOE_HALF, MOE_HALF))
            return carry
        lax.fori_loop(0, nch, body, 0)

    @pl.when((j == 0) & (nch > 0))
    def _():
        def clear(rows):
            xbuf[rows, :] = jnp.zeros((MOE_HALF, D_MODEL), xbuf.dtype)
        for_halves(clear)
        for_pieces(lambda *a: x_copy(*a).start())

        def init(rows):
            yacc[rows, :] = jnp.broadcast_to(bd_ref[...], (MOE_HALF, D_MODEL))
        for_halves(init)
        for_pieces(lambda *a: x_copy(*a).wait())

    @pl.when(nch > 0)
    def _():
        wgb[...] = wg_ref[...].astype(jnp.bfloat16)
        wub[...] = wu_ref[...].astype(jnp.bfloat16)
        wdb[...] = wd_ref[...].astype(jnp.bfloat16)

        def gate_up(rows):
            x = xbuf[rows, :]
            gate = jnp.dot(x, wgb[...], preferred_element_type=jnp.float32) + bg_ref[...]
            up = jnp.dot(x, wub[...], preferred_element_type=jnp.float32) + bu_ref[...]
            return gate, up

        def down(rows, gate, up):
            gate = jnp.minimum(gate, SWIGLU_LIMIT)
            up = jnp.clip(up, -SWIGLU_LIMIT, SWIGLU_LIMIT)
            act = (up + 1.0) * (gate / (1.0 + jnp.exp(-SWIGLU_ALPHA * gate)))
            yacc[rows, :] += jnp.dot(act.astype(jnp.bfloat16), wdb[...],
                                     preferred_element_type=jnp.float32)

        def run(row_slices):
            gu = gate_up(row_slices[0])
            for k, rows in enumerate(row_slices):
                nxt = gate_up(row_slices[k + 1]) if k + 1 < len(row_slices) else None
                down(rows, *gu)
                gu = nxt

        n_full = nch // 2

        def quad(qd, carry):
            run([chunk_rows(4 * qd + k) for k in range(4)])
            return carry
        lax.fori_loop(0, n_full // 4, quad, 0)

        @pl.when((n_full & 2) != 0)
        def _():
            base = n_full & -4
            run([chunk_rows(base), chunk_rows(base + 1)])

        @pl.when((n_full & 1) != 0)
        def _():
            run([chunk_rows(n_full - 1)])

        @pl.when((nch & 1) != 0)
        def _():
            run([rows_at(n_full * MOE_CHUNK, MOE_HALF)])

    @pl.when((j == nf - 1) & (nch > 0))
    def _():
        def stage(rows):
            xbuf[rows, :] = yacc[rows, :].astype(xbuf.dtype)
        for_halves(stage)
        for_pieces(lambda *a: y_copy(*a).start())
        for_pieces(lambda *a: y_copy(*a).wait())


N_FFN_PREFETCH = 8


def _expert_ffn(xs, tables, w_gu, b_gu, w_down, b_down, layer):
    nf = D_FF // MOE_TF

    def wspec(block, index):
        def index_map(s, j, sb_e, sb_i0, sb_i1, sb_nch, n_sb, *_):
            return index(sb_e[s], jnp.where(s < n_sb[0], j, nf - 1))
        return pl.BlockSpec(block, index_map)

    grid_spec = pltpu.PrefetchScalarGridSpec(
        num_scalar_prefetch=N_FFN_PREFETCH,
        grid=(N_SUPER, nf),
        in_specs=[
            pl.BlockSpec(memory_space=pl.ANY),
            wspec((None, None, D_MODEL, MOE_TF), lambda e, j: (layer, e, 0, j)),
            wspec((None, None, D_MODEL, MOE_TF), lambda e, j: (layer, e, 0, nf + j)),
            wspec((None, None, MOE_TF, D_MODEL), lambda e, j: (layer, e, j, 0)),
            wspec((None, None, 1, MOE_TF), lambda e, j: (layer, e, 0, j)),
            wspec((None, None, 1, MOE_TF), lambda e, j: (layer, e, 0, nf + j)),
            wspec((None, None, 1, D_MODEL), lambda e, j: (layer, e, 0, 0)),
        ],
        out_specs=pl.BlockSpec(memory_space=pl.ANY),
        scratch_shapes=[
            pltpu.VMEM((MOE_TMAX, D_MODEL), jnp.bfloat16),
            pltpu.VMEM((MOE_TMAX, D_MODEL), jnp.float32),
            pltpu.VMEM((D_MODEL, MOE_TF), jnp.bfloat16),
            pltpu.VMEM((D_MODEL, MOE_TF), jnp.bfloat16),
            pltpu.VMEM((MOE_TF, D_MODEL), jnp.bfloat16),
            pltpu.SemaphoreType.DMA((2,)),
        ],
    )
    b_gu4 = b_gu.reshape(DEPTH, N_EXPERTS, 1, 2 * D_FF)
    return pl.pallas_call(
        _ffn_kernel,
        out_shape=jax.ShapeDtypeStruct((N_SLOTS, D_MODEL), jnp.bfloat16),
        grid_spec=grid_spec,
        input_output_aliases={N_FFN_PREFETCH: 0},
        compiler_params=pltpu.CompilerParams(dimension_semantics=("arbitrary", "arbitrary"),
                                             vmem_limit_bytes=VMEM_LIMIT, has_side_effects=True),
        name="expert_ffn",
    )(*tables, xs, w_gu, w_gu, w_down, b_gu4, b_gu4, b_down.reshape(DEPTH, N_EXPERTS, 1, D_MODEL))


def _plan_superblocks(n_pad, off):
    i32 = jnp.int32
    cur = jnp.zeros((N_EXPERTS,), i32)
    n_closed = jnp.zeros((N_EXPERTS,), i32)
    local_sb, dst = [], []
    for i in range(MOE_NT):
        n = n_pad[i]
        overflow = cur + n > MOE_TMAX
        n_closed = n_closed + overflow.astype(i32)
        cur = jnp.where(overflow, n, cur + n)
        local_sb.append(n_closed)
        dst.append(cur - n)
    local_sb = jnp.stack(local_sb)
    dst = jnp.stack(dst)
    nsb_e = jnp.where(n_pad.sum(axis=0) > 0, n_closed + 1, 0)
    sb_end = jnp.cumsum(nsb_e)
    n_sb = sb_end[-1]
    group_sb = (sb_end - nsb_e)[None, :] + local_sb
    s = jnp.arange(N_SUPER, dtype=i32)
    e_of = jnp.minimum(jnp.searchsorted(sb_end, s, side='right'), N_EXPERTS - 1).astype(i32)
    live = s < n_sb
    mine = group_sb[:, e_of] == s[None, :]
    tile = jnp.arange(MOE_NT, dtype=i32)[:, None]
    i0 = jnp.min(jnp.where(mine, tile, MOE_NT), axis=0)
    i1 = jnp.max(jnp.where(mine, tile + 1, 0), axis=0)
    rows = jnp.sum(jnp.where(mine, n_pad[:, e_of], 0), axis=0)
    last_e = e_of[jnp.maximum(n_sb - 1, 0)]
    flat = lambda a: a.T.reshape(-1).astype(i32)
    return (jnp.where(live, e_of, last_e).astype(i32),
            jnp.where(live, i0, 0).astype(i32),
            jnp.where(live, i1, 0).astype(i32),
            jnp.where(live, (rows + MOE_HALF - 1) // MOE_HALF, 0).astype(i32),
            n_sb.reshape(1).astype(i32),
            flat(tile * MOE_SMAX + off), flat(n_pad), flat(dst))


def _combine_kernel(*refs, emit_h, gate_row, shift_row, scale_row):
    if emit_h:
        x_ref, ys_ref, ls_ref, g_ref, mod_ref, modn_ref, lng_ref, lnb_ref, xo_ref, h_ref = refs
    else:
        x_ref, ys_ref, ls_ref, g_ref, mod_ref, lng_ref, lnb_ref, xo_ref = refs
    ls = ls_ref[...]
    gts = g_ref[...]
    slot = lax.broadcasted_iota(jnp.int32, (ls.shape[0], MOE_SMAX), 1)
    weights = jnp.zeros(slot.shape, jnp.float32)
    for k in range(TOP_K):
        weights = jnp.where(slot == ls[:, k:k + 1], gts[:, k:k + 1], weights)
    sub = jnp.dot(weights.astype(jnp.bfloat16), ys_ref[...], preferred_element_type=jnp.float32)
    x = _post_norm(x_ref[...], sub, mod_ref[gate_row:gate_row + 1, :], lng_ref, lnb_ref)
    xo_ref[...] = x
    if emit_h:
        h_ref[...] = _modulate(x, modn_ref, shift_row, scale_row).astype(h_ref.dtype)


def _combine_norm(x, ys, ls, gates, ada4, layer, ln_g, ln_b, *, emit_h, tile_start=0, n_tiles=MOE_NT):
    tm = 256
    per = MOE_TT // tm
    in_rows = lambda i, r: ((i + tile_start) * per + r, 0)
    row_spec = pl.BlockSpec((tm, D_MODEL), in_rows)
    small_spec = pl.BlockSpec((tm, TOP_K), in_rows)
    out_spec = pl.BlockSpec((tm, D_MODEL), lambda i, r: (i * per + r, 0))
    in_specs = [row_spec, pl.BlockSpec((MOE_SMAX, D_MODEL), lambda i, r: (i + tile_start, 0)),
                small_spec, small_spec, _ada_spec(layer, tm, per, tile_start)]
    args = [x, ys, ls, gates, ada4]
    out_shape = [jax.ShapeDtypeStruct((n_tiles * MOE_TT, D_MODEL), jnp.float32)]
    out_specs = [out_spec]
    if emit_h:
        in_specs.append(_ada_spec(layer + 1, tm, per, tile_start))
        args.append(ada4)
        out_shape.append(jax.ShapeDtypeStruct((n_tiles * MOE_TT, D_MODEL), jnp.bfloat16))
        out_specs.append(out_spec)
    ln_specs, ln_arrs = _ln_args(ln_g, ln_b, layer, 1)
    kern = functools.partial(_combine_kernel, emit_h=emit_h, gate_row=5, shift_row=0, scale_row=1)
    return pl.pallas_call(
        kern, out_shape=out_shape, grid=(n_tiles, per), in_specs=in_specs + ln_specs,
        out_specs=out_specs, compiler_params=_cparams(("arbitrary", "arbitrary")), name="combine",
    )(*args, *ln_arrs)


def kernel(x_prompt, x_sample, cache_diff_k, cache_diff_v, cache_win_k, cache_win_v, c, c_ctx, w_ada, b_ada, ln_g, ln_b, diff_wq, diff_wk, diff_wv, diff_wo, diff_lambda, diff_subln_g, win_wq, win_wk, win_wv, win_wo, win_sink, pool_w, pool_b, pool_scale, moe_w_router, moe_b_router, moe_w_gu, moe_b_gu, moe_w_down, moe_b_down):
    f32, bf16 = jnp.float32, jnp.bfloat16
    x = (x_prompt.reshape(T_P, D_MODEL), x_sample.reshape(T_S, D_MODEL))
    cvec = jnp.concatenate([c_ctx[None, :], c, jnp.zeros((COND_ROWS - N_COND, D_MODEL), f32)], axis=0)
    ada4 = _ada_all(cvec, w_ada, b_ada).reshape(DEPTH, COND_ROWS, N_ADA, D_MODEL)
    rope_a = _rope_tables(DK_A)
    rope_b = _rope_tables(DH_B)
    n_la = cache_diff_k.shape[1]
    n_lb = cache_win_k.shape[1]
    cdk = cache_diff_k.reshape(DEC_BATCH, n_la, PAST_LEN, D_MODEL)
    cdv = cache_diff_v.reshape(DEC_BATCH, n_la, PAST_LEN, D_MODEL)
    cwk = cache_win_k.reshape(DEC_BATCH, n_lb, PAST_LEN, KV_B * DH_B)
    cwv = cache_win_v.reshape(DEC_BATCH, n_lb, PAST_LEN, KV_B * DH_B)

    (h,) = _modnorm(x, None, ada4, 0, ln_g, ln_b, h_dtype=bf16, shift_row=0, scale_row=1)
    st_dk, st_dv, st_wk, st_wv = [], [], [], []
    ia = ib = ic = 0
    for i in range(DEPTH):
        kind = i % N_MIXERS
        if kind == 0:
            lam_init = 0.8 - 0.6 * math.exp(-0.3 * i)
            pj = functools.partial(_matmul, h, layer=ia)
            q_p = pj(diff_wq, row_start=0, n_rows=T_P, out_dtype=bf16)
            k_p = pj(diff_wk, row_start=0, n_rows=T_P, out_dtype=f32)
            v_p = pj(diff_wv, row_start=0, n_rows=T_P, out_dtype=f32)
            q_s = pj(diff_wq, row_start=T_P, n_rows=T_S, out_dtype=bf16, rope=rope_a)
            k_s = pj(diff_wk, row_start=T_P, n_rows=T_S, out_dtype=bf16, rope=rope_a)
            v_s = pj(diff_wv, row_start=T_P, n_rows=T_S, out_dtype=bf16)
            seq_spec = pl.BlockSpec((SEQ, DV_A), lambda b, hh, qi: (b, hh))
            o = _diff_attention(q_p, [(k_p, seq_spec)], [(v_p, seq_spec)], diff_lambda, diff_subln_g,
                                ia, lam_init, n_batch=BATCH, n_q=SEQ, tq=SEQ, out_row_start=0,
                                prev_out=None)
            cache_spec = pl.BlockSpec((None, None, PAST_LEN, DV_A), lambda b, hh, qi, ia=ia: (b, ia, 0, hh))
            lat_spec = pl.BlockSpec((DEC_SEQ, DV_A), lambda b, hh, qi: (b, hh))
            o = _diff_attention(q_s, [(cdk, cache_spec), (k_s, lat_spec)],
                                [(cdv, cache_spec), (v_s, lat_spec)], diff_lambda, diff_subln_g,
                                ia, lam_init, n_batch=DEC_BATCH, n_q=DEC_SEQ, tq=256,
                                out_row_start=T_P, prev_out=o)
            sub = _matmul(o, diff_wo, ia, row_start=0, n_rows=T_ALL, out_dtype=f32)
            st_dk.append(k_p.reshape(BATCH, SEQ, H_A, 2, DK_A))
            st_dv.append(v_p.reshape(BATCH, SEQ, H_A, DV_A))
            ia += 1
        elif kind == 1:
            pj = functools.partial(_matmul, h, layer=ib)
            q_p = pj(win_wq, row_start=0, n_rows=T_P, out_dtype=bf16)
            k_p = pj(win_wk, row_start=0, n_rows=T_P, out_dtype=f32)
            v_p = pj(win_wv, row_start=0, n_rows=T_P, out_dtype=f32)
            q_s = pj(win_wq, row_start=T_P, n_rows=T_S, out_dtype=bf16, rope=rope_b)
            k_s = pj(win_wk, row_start=T_P, n_rows=T_S, out_dtype=bf16, rope=rope_b)
            v_s = pj(win_wv, row_start=T_P, n_rows=T_S, out_dtype=bf16)
            kvw = KV_B * DH_B
            seq_spec = pl.BlockSpec((SEQ, kvw), lambda b, qi: (b, 0))
            o = _gqa_attention(q_p, [(k_p, seq_spec)], [(v_p, seq_spec)], win_sink, ib,
                               n_batch=BATCH, n_q=SEQ, tq=SEQ, band=0, out_row_start=0, prev_out=None)
            cache_spec = pl.BlockSpec((None, None, PAST_LEN, kvw), lambda b, qi, ib=ib: (b, ib, 0, 0))
            lat_spec = pl.BlockSpec((DEC_SEQ, kvw), lambda b, qi: (b, 0))
            o = _gqa_attention(q_s, [(cwk, cache_spec), (k_s, lat_spec)],
                               [(cwv, cache_spec), (v_s, lat_spec)], win_sink, ib,
                               n_batch=DEC_BATCH, n_q=DEC_SEQ, tq=WINDOW, band=3 * WINDOW,
                               out_row_start=T_P, prev_out=o)
            sub = _matmul(o, win_wo, ib, row_start=0, n_rows=T_ALL, out_dtype=f32)
            st_wk.append(k_p.reshape(BATCH, SEQ, KV_B, DH_B))
            st_wv.append(v_p.reshape(BATCH, SEQ, KV_B, DH_B))
            ib += 1
        else:
            sub = _pool_mixer(h, pool_w, pool_b, pool_scale, ic)
            ic += 1
        x, h = _modnorm(x, sub, ada4, i, ln_g, ln_b, h_dtype=f32, gate_row=2, shift_row=3, scale_row=4)
        gates, ls, n_pad, off, xs = _route(h, moe_w_router, moe_b_router, i)
        ys = _expert_ffn(xs, _plan_superblocks(n_pad, off), moe_w_gu, moe_b_gu, moe_w_down,
                         moe_b_down, i)
        if i < DEPTH - 1:
            x, h = _combine_norm(x, ys, ls, gates, ada4, i, ln_g, ln_b, emit_h=True)
    tiles_p = T_P // MOE_TT
    last = functools.partial(_combine_norm, x, ys, ls, gates, ada4, DEPTH - 1, ln_g, ln_b, emit_h=False)
    (y_p,) = last(tile_start=0, n_tiles=tiles_p)
    (y_s,) = last(tile_start=tiles_p, n_tiles=MOE_NT - tiles_p)
    y_prompt = y_p.reshape(BATCH, SEQ, D_MODEL)
    y_sample = y_s.reshape(DEC_BATCH, DEC_SEQ, D_MODEL)
    return (y_prompt, y_sample, jnp.stack(st_dk, axis=1), jnp.stack(st_dv, axis=1),
            jnp.stack(st_wk, axis=1), jnp.stack(st_wv, axis=1))
```

```python
import functools
import math

import jax
import jax.numpy as jnp
from jax import lax
from jax.experimental import pallas as pl
from jax.experimental.pallas import tpu as pltpu

D_MODEL = 2048
BATCH = 16
SEQ = 256
DEPTH = 4
DEC_BATCH = 2
DEC_SEQ = 2048
PAST_LEN = 512
GRID_W = 64
N_MIXERS = 3
H_A = 8
DK_A = D_MODEL // (2 * H_A)
DV_A = 2 * DK_A
DH_B = 64
H_B = D_MODEL // DH_B
KV_B = 4
G_B = H_B // KV_B
WINDOW = 128
POOL_SIZES = (2, 4, 8, 16)
POOL_GROUP = D_MODEL // len(POOL_SIZES)
N_EXPERTS = 32
TOP_K = 4
D_FF = D_MODEL
SWIGLU_ALPHA = 1.702
SWIGLU_LIMIT = 7.0
ROPE_BASE = 10000.0
LN_EPS = 1e-5
DEEPNORM_ALPHA = (2 * DEPTH) ** 0.25
LOG2_E = math.log2(math.e)
N_ADA = 6

T_P = BATCH * SEQ
T_S = DEC_BATCH * DEC_SEQ
T_ALL = T_P + T_S
N_COND = 1 + DEC_BATCH
COND_ROWS = 8

LANES = 128
VMEM_LIMIT = 52 * 1024 * 1024
BF16_ROWS = 16
MOE_TT = 512
MOE_NT = T_ALL // MOE_TT
MOE_ALIGN = BF16_ROWS
MOE_SMAX = MOE_TT * TOP_K + N_EXPERTS * MOE_ALIGN
N_SLOTS = MOE_NT * MOE_SMAX
MOE_CHUNK = 256
MOE_HALF = MOE_CHUNK // 2
MOE_TMAX = 2048
MOE_TF = 256
N_SUPER = N_EXPERTS + N_SLOTS // (MOE_TMAX - MOE_TT)
DMA_SIZES = (512, 256, 128, 64, 32, 16)


def _cparams(sem, vmem=VMEM_LIMIT):
    return pltpu.CompilerParams(dimension_semantics=sem, vmem_limit_bytes=vmem)


def _cond_group(row):
    return jnp.maximum(row // DEC_SEQ - (T_P // DEC_SEQ - 1), 0)


def _ada_kernel(c_ref, w_ref, b_ref, o_ref):
    cv = c_ref[...]
    s = (cv / (1.0 + jnp.exp(-cv))).astype(jnp.bfloat16)
    o_ref[...] = jnp.dot(s, w_ref[...].astype(jnp.bfloat16),
                         preferred_element_type=jnp.float32) + b_ref[...]


def _ada_all(cvec, w_ada, b_ada):
    tn = 1024
    n = N_ADA * D_MODEL
    return pl.pallas_call(
        _ada_kernel,
        out_shape=jax.ShapeDtypeStruct((DEPTH, COND_ROWS, n), jnp.float32),
        grid=(DEPTH, n // tn),
        in_specs=[
            pl.BlockSpec((COND_ROWS, D_MODEL), lambda l, j: (0, 0)),
            pl.BlockSpec((None, D_MODEL, tn), lambda l, j: (l, 0, j)),
            pl.BlockSpec((None, 1, tn), lambda l, j: (l, 0, j)),
        ],
        out_specs=pl.BlockSpec((None, COND_ROWS, tn), lambda l, j: (l, 0, j)),
        compiler_params=_cparams(("arbitrary", "arbitrary")),
        name="ada",
    )(cvec, w_ada, b_ada.reshape(DEPTH, 1, n))


def _post_norm(x, sub, gate, lng_ref, lnb_ref):
    xf = DEEPNORM_ALPHA * x + gate * sub
    mu = jnp.mean(xf, axis=-1, keepdims=True)
    xc = xf - mu
    var = jnp.mean(xc * xc, axis=-1, keepdims=True)
    return xc * lax.rsqrt(var + LN_EPS) * lng_ref[...] + lnb_ref[...]


def _modulate(x, mod_ref, shift_row, scale_row):
    return x * (1.0 + mod_ref[scale_row:scale_row + 1, :]) + mod_ref[shift_row:shift_row + 1, :]


def _modnorm_kernel(*refs, do_norm, split_tiles, gate_row, shift_row, scale_row):
    if split_tiles:
        xp_ref, xs_ref, *refs = refs
        x = jnp.where(pl.program_id(0) < split_tiles, xp_ref[...], xs_ref[...])
    else:
        x_ref, *refs = refs
        x = x_ref[...]
    if do_norm:
        sub_ref, mod_ref, lng_ref, lnb_ref, xo_ref, h_ref = refs
        x = _post_norm(x, sub_ref[...], mod_ref[gate_row:gate_row + 1, :], lng_ref, lnb_ref)
        xo_ref[...] = x
    else:
        mod_ref, h_ref = refs
    h_ref[...] = _modulate(x, mod_ref, shift_row, scale_row).astype(h_ref.dtype)


def _ada_spec(layer, tm, rows_per_step=1, step_start=0):
    return pl.BlockSpec(
        (None, None, N_ADA, D_MODEL),
        lambda i, *_: (layer, _cond_group((i + step_start) * tm * rows_per_step), 0, 0))


def _ln_args(ln_g, ln_b, layer, which):
    spec = pl.BlockSpec((None, None, 1, D_MODEL), lambda i, *_: (layer, which, 0, 0))
    return [spec, spec], [ln_g.reshape(DEPTH, 2, 1, D_MODEL), ln_b.reshape(DEPTH, 2, 1, D_MODEL)]


def _modnorm(x, sub, ada4, layer, ln_g, ln_b, *, h_dtype, gate_row=0, shift_row=0, scale_row=0):
    tm = 256
    do_norm = sub is not None
    row_spec = pl.BlockSpec((tm, D_MODEL), lambda i: (i, 0))
    split_tiles = 0
    if isinstance(x, tuple):
        split_tiles = T_P // tm
        in_specs = [pl.BlockSpec((tm, D_MODEL), lambda i: (jnp.minimum(i, split_tiles - 1), 0)),
                    pl.BlockSpec((tm, D_MODEL), lambda i: (jnp.maximum(i - split_tiles, 0), 0))]
        args = list(x)
    else:
        in_specs, args = [row_spec], [x]
    if do_norm:
        in_specs.append(row_spec)
        args.append(sub)
    in_specs.append(_ada_spec(layer, tm))
    args.append(ada4)
    out_shape = [jax.ShapeDtypeStruct((T_ALL, D_MODEL), h_dtype)]
    out_specs = [row_spec]
    if do_norm:
        ln_specs, ln_arrs = _ln_args(ln_g, ln_b, layer, 0)
        in_specs += ln_specs
        args += ln_arrs
        out_shape.insert(0, jax.ShapeDtypeStruct((T_ALL, D_MODEL), jnp.float32))
        out_specs.insert(0, row_spec)
    kern = functools.partial(_modnorm_kernel, do_norm=do_norm, split_tiles=split_tiles,
                             gate_row=gate_row, shift_row=shift_row, scale_row=scale_row)
    return pl.pallas_call(
        kern, out_shape=out_shape, grid=(T_ALL // tm,), in_specs=in_specs, out_specs=out_specs,
        compiler_params=_cparams(("arbitrary",)), name="modnorm",
    )(*args)


def _mm_kernel(*refs, rope_quarter, has_rope):
    if has_rope:
        a_ref, w_ref, cos_ref, sin_ref, o_ref, wb_ref = refs
    else:
        a_ref, w_ref, o_ref, wb_ref = refs

    @pl.when(pl.program_id(1) == 0)
    def _():
        wb_ref[...] = w_ref[...].astype(jnp.bfloat16)

    acc = jnp.dot(a_ref[...], wb_ref[...], preferred_element_type=jnp.float32)
    if has_rope:
        q = rope_quarter
        cs = cos_ref[...]
        sn = sin_ref[...]
        lane = lax.broadcasted_iota(jnp.int32, (acc.shape[0], LANES), 1)
        first = (lane % (2 * q)) < q
        for c in range(acc.shape[1] // LANES):
            a = acc[:, c * LANES:(c + 1) * LANES]
            partner = jnp.where(first, pltpu.roll(a, LANES - q, 1), pltpu.roll(a, q, 1))
            o_ref[:, c * LANES:(c + 1) * LANES] = (a * cs + partner * sn).astype(o_ref.dtype)
    else:
        o_ref[...] = acc.astype(o_ref.dtype)


def _matmul(a, w, layer, *, row_start, n_rows, out_dtype, rope=None, tm=1024, tn=512):
    k = a.shape[1]
    n = w.shape[2]
    tn = min(tn, n)
    m_off = row_start // tm
    in_specs = [
        pl.BlockSpec((tm, k), lambda j, i: (i + m_off, 0)),
        pl.BlockSpec((None, k, tn), lambda j, i: (layer, 0, j)),
    ]
    args = [a, w]
    has_rope = rope is not None
    quarter = 0
    if has_rope:
        cos_t, sin_t, quarter = rope
        per = DEC_SEQ // tm
        in_specs += [pl.BlockSpec((tm, LANES), lambda j, i: (i % per, 0))] * 2
        args += [cos_t, sin_t]
    return pl.pallas_call(
        functools.partial(_mm_kernel, rope_quarter=quarter, has_rope=has_rope),
        out_shape=jax.ShapeDtypeStruct((n_rows, n), out_dtype),
        grid=(n // tn, n_rows // tm),
        in_specs=in_specs,
        out_specs=pl.BlockSpec((tm, tn), lambda j, i: (i, j)),
        scratch_shapes=[pltpu.VMEM((k, tn), jnp.bfloat16)],
        compiler_params=_cparams(("arbitrary", "arbitrary")),
        name="proj",
    )(*args)


def _rope_tables(head_dim):
    quarter = head_dim // 4
    pos = jnp.arange(DEC_SEQ)
    row = (pos // GRID_W).astype(jnp.float32)
    col = (pos % GRID_W).astype(jnp.float32)
    inv = ROPE_BASE ** (-jnp.arange(quarter, dtype=jnp.float32) / quarter)
    ang_r = row[:, None] * inv
    ang_c = col[:, None] * inv
    ang = jnp.concatenate([ang_r, ang_r, ang_c, ang_c], axis=1)
    sign = jnp.concatenate([-jnp.ones(quarter), jnp.ones(quarter)] * 2)
    reps = LANES // head_dim
    cos_t = jnp.tile(jnp.cos(ang), (1, reps))
    sin_t = jnp.tile(jnp.sin(ang) * sign[None, :], (1, reps))
    return cos_t.astype(jnp.float32), sin_t.astype(jnp.float32), quarter


def _diff_attn_kernel(*refs, n_parts, lam_init):
    q_ref = refs[0]
    k_refs = refs[1:1 + n_parts]
    v_refs = refs[1 + n_parts:1 + 2 * n_parts]
    lam_ref, g_ref, o_ref = refs[1 + 2 * n_parts:]
    scale = DK_A ** -0.5
    lp = lam_ref[...]
    lam = (jnp.exp(jnp.sum(lp[0:1] * lp[1:2], axis=-1, keepdims=True))
           - jnp.exp(jnp.sum(lp[2:3] * lp[3:4], axis=-1, keepdims=True)) + lam_init)
    q = q_ref[...]
    v_vals = [vr[...].astype(jnp.bfloat16) for vr in v_refs]
    outs = []
    for c in range(2):
        qc = q[:, c * DK_A:(c + 1) * DK_A]
        s_parts = []
        for kr in k_refs:
            kc = kr[:, c * DK_A:(c + 1) * DK_A].astype(jnp.bfloat16)
            s_parts.append(lax.dot_general(qc, kc, (((1,), (1,)), ((), ())),
                                           preferred_element_type=jnp.float32))
        m = s_parts[0].max(axis=-1, keepdims=True)
        for s in s_parts[1:]:
            m = jnp.maximum(m, s.max(axis=-1, keepdims=True))
        e_parts = [jnp.exp2((s - m) * (scale * LOG2_E)) for s in s_parts]
        den = e_parts[0].sum(axis=-1, keepdims=True)
        for e in e_parts[1:]:
            den = den + e.sum(axis=-1, keepdims=True)
        oc = None
        for e, v in zip(e_parts, v_vals):
            contrib = jnp.dot(e.astype(jnp.bfloat16), v, preferred_element_type=jnp.float32)
            oc = contrib if oc is None else oc + contrib
        outs.append(oc / den)
    o = outs[0] - lam * outs[1]
    ms = jnp.mean(o * o, axis=-1, keepdims=True)
    o = o * lax.rsqrt(ms + LN_EPS) * g_ref[...] * (1.0 - lam_init)
    o_ref[...] = o.astype(o_ref.dtype)


def _diff_attention(q, k_parts, v_parts, lam_p, subln_g, ia, lam_init, *, n_batch, n_q, tq,
                    out_row_start, prev_out):
    nqb = n_q // tq
    row_off = out_row_start // tq
    in_specs = [pl.BlockSpec((tq, DV_A), lambda b, h, i: (b * nqb + i, h))]
    args = [q]
    for arr, spec in k_parts + v_parts:
        in_specs.append(spec)
        args.append(arr)
    in_specs.append(pl.BlockSpec((None, 4, DK_A), lambda b, h, i: (ia, 0, 0)))
    in_specs.append(pl.BlockSpec((None, 1, DV_A), lambda b, h, i: (ia, 0, 0)))
    args += [lam_p, subln_g.reshape(-1, 1, DV_A)]
    aliases = {}
    if prev_out is not None:
        in_specs.append(pl.BlockSpec(memory_space=pl.ANY))
        args.append(prev_out)
        aliases = {len(args) - 1: 0}

    def kern(*refs):
        if prev_out is not None:
            refs = refs[:-2] + refs[-1:]
        _diff_attn_kernel(*refs, n_parts=len(k_parts), lam_init=lam_init)

    return pl.pallas_call(
        kern,
        out_shape=jax.ShapeDtypeStruct((T_ALL, D_MODEL), jnp.bfloat16),
        grid=(n_batch, H_A, nqb),
        in_specs=in_specs,
        out_specs=pl.BlockSpec((tq, DV_A), lambda b, h, i: (row_off + b * nqb + i, h)),
        input_output_aliases=aliases,
        compiler_params=_cparams(("arbitrary", "arbitrary", "arbitrary")),
        name="diff_attn",
    )(*args)


def _gqa_kernel(*refs, n_parts, band, tq, n_lat):
    q_ref = refs[0]
    k_refs = refs[1:1 + n_parts]
    v_refs = refs[1 + n_parts:1 + 2 * n_parts]
    sink_ref, o_ref = refs[1 + 2 * n_parts:]
    scale = DH_B ** -0.5
    i = pl.program_id(1)
    q = q_ref[...]
    sink = sink_ref[...]
    grp = lax.broadcasted_iota(jnp.int32, (G_B * tq, 1), 0) // tq
    if band:
        start = pl.multiple_of(jnp.clip(i * tq - WINDOW, 0, n_lat - band), WINDOW)
        qpos = i * tq + (lax.broadcasted_iota(jnp.int32, (G_B * tq, band), 0) & (tq - 1))
        kpos = start + lax.broadcasted_iota(jnp.int32, (G_B * tq, band), 1)
        valid = jnp.abs(qpos - kpos) <= WINDOW
    k_vals, v_vals = [], []
    for p in range(n_parts):
        if band and p == n_parts - 1:
            k_vals.append(k_refs[p][pl.ds(start, band), :].astype(jnp.bfloat16))
            v_vals.append(v_refs[p][pl.ds(start, band), :].astype(jnp.bfloat16))
        else:
            k_vals.append(k_refs[p][...].astype(jnp.bfloat16))
            v_vals.append(v_refs[p][...].astype(jnp.bfloat16))
    for kk in range(KV_B):
        heads = [kk * G_B + g for g in range(G_B)]
        qs = jnp.concatenate([q[:, h * DH_B:(h + 1) * DH_B] for h in heads], axis=0)
        sk = jnp.zeros((G_B * tq, 1), jnp.float32)
        for g, h in enumerate(heads):
            sk = jnp.where(grp == g, sink[:, h:h + 1], sk)
        s_parts = []
        for p in range(n_parts):
            kc = k_vals[p][:, kk * DH_B:(kk + 1) * DH_B]
            s = lax.dot_general(qs, kc, (((1,), (1,)), ((), ())),
                                preferred_element_type=jnp.float32)
            if band and p == n_parts - 1:
                s = jnp.where(valid, s, -1e30)
            s_parts.append(s)
        m = sk * (1.0 / scale)
        sk_raw = m
        for s in s_parts:
            m = jnp.maximum(m, s.max(axis=-1, keepdims=True))
        e_parts = [jnp.exp2((s - m) * (scale * LOG2_E)) for s in s_parts]
        den = jnp.exp2((sk_raw - m) * (scale * LOG2_E))
        for e in e_parts:
            den = den + e.sum(axis=-1, keepdims=True)
        o = None
        for p in range(n_parts):
            contrib = jnp.dot(e_parts[p].astype(jnp.bfloat16), v_vals[p][:, kk * DH_B:(kk + 1) * DH_B],
                              preferred_element_type=jnp.float32)
            o = contrib if o is None else o + contrib
        o = o / den
        o_ref[:, kk * G_B * DH_B:(kk + 1) * G_B * DH_B] = jnp.concatenate(
            [o[g * tq:(g + 1) * tq] for g in range(G_B)], axis=1).astype(o_ref.dtype)


def _gqa_attention(q, k_parts, v_parts, sink, ib, *, n_batch, n_q, tq, band, out_row_start, prev_out):
    nqb = n_q // tq
    row_off = out_row_start // tq
    in_specs = [pl.BlockSpec((tq, D_MODEL), lambda b, i: (b * nqb + i, 0))]
    args = [q]
    for arr, spec in k_parts + v_parts:
        in_specs.append(spec)
        args.append(arr)
    in_specs.append(pl.BlockSpec((None, 1, H_B), lambda b, i: (ib, 0, 0)))
    args.append(sink.reshape(-1, 1, H_B))
    aliases = {}
    if prev_out is not None:
        in_specs.append(pl.BlockSpec(memory_space=pl.ANY))
        args.append(prev_out)
        aliases = {len(args) - 1: 0}

    def kern(*refs):
        if prev_out is not None:
            refs = refs[:-2] + refs[-1:]
        _gqa_kernel(*refs, n_parts=len(k_parts), band=band, tq=tq, n_lat=n_q)

    return pl.pallas_call(
        kern,
        out_shape=jax.ShapeDtypeStruct((T_ALL, D_MODEL), jnp.bfloat16),
        grid=(n_batch, nqb),
        in_specs=in_specs,
        out_specs=pl.BlockSpec((tq, D_MODEL), lambda b, i: (row_off + b * nqb + i, 0)),
        input_output_aliases=aliases,
        compiler_params=_cparams(("arbitrary", "arbitrary")),
        name="gqa_attn",
    )(*args)


def _pool_kernel(h_ref, w_ref, b_ref, ls_ref, o_ref, *, rows):
    i = pl.program_id(0)
    g = pl.program_id(1)
    seq = jnp.where(i * rows < T_P, SEQ, DEC_SEQ)
    pos = lax.broadcasted_iota(jnp.int32, (rows, 1), 0) & (seq - 1)

    def shifted(a, k):
        ok = (pos + k >= 0) & (pos + k < seq)
        return jnp.where(ok, pltpu.roll(a, (-k) % rows, 0), 0.0)

    for gi, wsz in enumerate(POOL_SIZES):
        @pl.when(g == gi)
        def _(wsz=wsz):
            h = h_ref[...].astype(jnp.float32)
            half = wsz // 2
            fwd = h
            m = 1
            while m < half:
                fwd = fwd + shifted(fwd, m)
                m *= 2
            bwd = shifted(h, -1)
            m = 1
            while m < half:
                bwd = bwd + shifted(bwd, -m)
                m *= 2
            lo = jnp.maximum(pos - half, 0)
            hi = jnp.minimum(pos + half - 1, seq - 1)
            cnt = (hi - lo + 1).astype(jnp.float32)
            diff = ((fwd + bwd) / cnt - h).astype(jnp.bfloat16)
            y = jnp.dot(diff, w_ref[...].astype(jnp.bfloat16), preferred_element_type=jnp.float32)
            o_ref[...] = (y + b_ref[...]) * ls_ref[...]


def _pool_mixer(h, pool_w, pool_b, pool_scale, ic):
    rows = DEC_SEQ
    ng = len(POOL_SIZES)
    return pl.pallas_call(
        functools.partial(_pool_kernel, rows=rows),
        out_shape=jax.ShapeDtypeStruct((T_ALL, D_MODEL), jnp.float32),
        grid=(T_ALL // rows, ng),
        in_specs=[
            pl.BlockSpec((rows, POOL_GROUP), lambda i, g: (i, g)),
            pl.BlockSpec((None, None, POOL_GROUP, POOL_GROUP), lambda i, g: (ic, g, 0, 0)),
            pl.BlockSpec((None, None, 1, POOL_GROUP), lambda i, g: (ic, g, 0, 0)),
            pl.BlockSpec((None, None, 1, POOL_GROUP), lambda i, g: (ic, g, 0, 0)),
        ],
        out_specs=pl.BlockSpec((rows, POOL_GROUP), lambda i, g: (i, g)),
        compiler_params=_cparams(("arbitrary", "arbitrary")),
        name="pool",
    )(h, pool_w, pool_b.reshape(-1, ng, 1, POOL_GROUP), pool_scale.reshape(-1, ng, 1, POOL_GROUP))


def _route_kernel(h_ref, w_ref, b_ref, g_ref, ls_ref, n_ref, off_ref, xs_ref):
    tt = h_ref.shape[0]
    h = h_ref[...]
    logits = jnp.dot(h, w_ref[...], precision=lax.Precision.HIGHEST,
                     preferred_element_type=jnp.float32) + b_ref[...]
    lane = lax.broadcasted_iota(jnp.int32, logits.shape, 1).astype(jnp.float32)
    work = logits
    vals, idxs = [], []
    member = jnp.zeros(logits.shape, jnp.float32)
    for _ in range(TOP_K):
        m = work.max(axis=-1, keepdims=True)
        idx = jnp.where(work == m, lane, float(N_EXPERTS)).min(axis=-1, keepdims=True)
        hit = lane == idx
        member = jnp.where(hit, 1.0, member)
        work = jnp.where(hit, -jnp.inf, work)
        vals.append(m)
        idxs.append(idx)
    es = [jnp.exp(v - vals[0]) for v in vals]
    den = es[0]
    for e in es[1:]:
        den = den + e
    r_i = lax.broadcasted_iota(jnp.int32, (tt, tt), 0)
    c_i = lax.broadcasted_iota(jnp.int32, (tt, tt), 1)
    tri = jnp.where(c_i < r_i, 1.0, 0.0).astype(jnp.bfloat16)
    before = jnp.dot(tri, member.astype(jnp.bfloat16), preferred_element_type=jnp.float32)
    n_units = jnp.floor((member.sum(axis=0, keepdims=True) + (MOE_ALIGN - 1)) * (1.0 / MOE_ALIGN))
    e_r = lax.broadcasted_iota(jnp.int32, (N_EXPERTS, N_EXPERTS), 0)
    e_c = lax.broadcasted_iota(jnp.int32, (N_EXPERTS, N_EXPERTS), 1)
    upper = jnp.where(e_r < e_c, 1.0, 0.0).astype(jnp.bfloat16)
    off_units = jnp.dot(jnp.broadcast_to(n_units, (8, N_EXPERTS)).astype(jnp.bfloat16), upper,
                        preferred_element_type=jnp.float32)[0:1]
    n_ref[...] = (n_units * MOE_ALIGN).astype(jnp.int32)
    off_ref[...] = (off_units * MOE_ALIGN).astype(jnp.int32)
    slot_of = off_units * MOE_ALIGN + before
    lane128 = lax.broadcasted_iota(jnp.int32, (tt, LANES), 1)
    ls_mat = jnp.full((tt, LANES), -1.0, jnp.float32)
    for k in range(TOP_K):
        ls_k = jnp.where(lane == idxs[k], slot_of, 0.0).sum(axis=-1, keepdims=True)
        g_ref[:, k:k + 1] = es[k] / den
        ls_ref[:, k:k + 1] = ls_k.astype(jnp.int32)
        ls_mat = jnp.where(lane128 == k, ls_k, ls_mat)
    ls_t = ls_mat.T
    hb = h.astype(jnp.bfloat16)
    for c in range(MOE_SMAX // tt):
        srow = (lax.broadcasted_iota(jnp.int32, (tt, tt), 0) + c * tt).astype(jnp.float32)
        sel = jnp.zeros((tt, tt), jnp.float32)
        for k in range(TOP_K):
            sel = jnp.where(srow == ls_t[k:k + 1, :], 1.0, sel)
        xs_ref[c * tt:(c + 1) * tt, :] = jnp.dot(
            sel.astype(jnp.bfloat16), hb, preferred_element_type=jnp.float32).astype(xs_ref.dtype)


def _route(h, w_router, b_router, layer):
    small = lambda dt: jax.ShapeDtypeStruct((T_ALL, TOP_K), dt)
    per_tile = jax.ShapeDtypeStruct((MOE_NT, 1, N_EXPERTS), jnp.int32)
    gates, ls, n_pad, off, xs = pl.pallas_call(
        _route_kernel,
        out_shape=[small(jnp.float32), small(jnp.int32), per_tile, per_tile,
                   jax.ShapeDtypeStruct((N_SLOTS, D_MODEL), jnp.bfloat16)],
        grid=(MOE_NT,),
        in_specs=[
            pl.BlockSpec((MOE_TT, D_MODEL), lambda i: (i, 0)),
            pl.BlockSpec((None, D_MODEL, N_EXPERTS), lambda i: (layer, 0, 0)),
            pl.BlockSpec((None, 1, N_EXPERTS), lambda i: (layer, 0, 0)),
        ],
        out_specs=[pl.BlockSpec((MOE_TT, TOP_K), lambda i: (i, 0))] * 2
        + [pl.BlockSpec((None, 1, N_EXPERTS), lambda i: (i, 0, 0))] * 2
        + [pl.BlockSpec((MOE_SMAX, D_MODEL), lambda i: (i, 0))],
        compiler_params=_cparams(("arbitrary",)),
        name="route",
    )(h, w_router, b_router.reshape(DEPTH, 1, N_EXPERTS))
    return gates, ls, n_pad.reshape(MOE_NT, N_EXPERTS), off.reshape(MOE_NT, N_EXPERTS), xs


def _ffn_kernel(sb_e, sb_i0, sb_i1, sb_nch, n_sb, g_src, g_n, g_dst,
                xs_ref, wg_ref, wu_ref, wd_ref, bg_ref, bu_ref, bd_ref,
                ys_ref, xbuf, yacc, wgb, wub, wdb, sem):
    del n_sb
    s = pl.program_id(0)
    j = pl.program_id(1)
    nf = pl.num_programs(1)
    nch = sb_nch[s]

    def for_pieces(fn):
        def body(i, carry):
            g = sb_e[s] * MOE_NT + i
            n = g_n[g]
            for size in DMA_SIZES:
                done = n & (-2 * size)

                @pl.when((n & size) != 0)
                def _(size=size, done=done):
                    fn(pl.multiple_of(g_src[g] + done, MOE_ALIGN),
                       pl.multiple_of(g_dst[g] + done, MOE_ALIGN), size)
            return carry
        lax.fori_loop(sb_i0[s], sb_i1[s], body, 0)

    def x_copy(hbm_row, vmem_row, size):
        return pltpu.make_async_copy(xs_ref.at[pl.ds(hbm_row, size)],
                                     xbuf.at[pl.ds(vmem_row, size)], sem.at[0])

    def y_copy(hbm_row, vmem_row, size):
        return pltpu.make_async_copy(xbuf.at[pl.ds(vmem_row, size)],
                                     ys_ref.at[pl.ds(hbm_row, size)], sem.at[1])

    def rows_at(row, size):
        return pl.ds(pl.multiple_of(row, MOE_HALF), size)

    def chunk_rows(c):
        return rows_at(c * MOE_CHUNK, MOE_CHUNK)

    def for_halves(fn):
        def body(c, carry):
            fn(rows_at(c * MOE_HALF, MOE_HALF))
            return carry
        lax.fori_loop(0, nch, body, 0)

    @pl.when((j == 0) & (nch > 0))
    def _():
        def clear(rows):
            xbuf[rows, :] = jnp.zeros((MOE_HALF, D_MODEL), xbuf.dtype)
        for_halves(clear)
        for_pieces(lambda *a: x_copy(*a).start(priority=1))

        def init(rows):
            yacc[rows, :] = jnp.broadcast_to(bd_ref[...], (MOE_HALF, D_MODEL))
        for_halves(init)
        for_pieces(lambda *a: x_copy(*a).wait())

    @pl.when(nch > 0)
    def _():
        wgb[...] = wg_ref[...].astype(jnp.bfloat16)
        wub[...] = wu_ref[...].astype(jnp.bfloat16)
        wdb[...] = wd_ref[...].astype(jnp.bfloat16)

        def gate_up(rows):
            x = xbuf[rows, :]
            gate = jnp.dot(x, wgb[...], preferred_element_type=jnp.float32) + bg_ref[...]
            up = jnp.dot(x, wub[...], preferred_element_type=jnp.float32) + bu_ref[...]
            return gate, up

        def down(rows, gate, up):
            gate = jnp.minimum(gate, SWIGLU_LIMIT)
            up = jnp.clip(up, -SWIGLU_LIMIT, SWIGLU_LIMIT)
            act = (up + 1.0) * (gate / (1.0 + jnp.exp(-SWIGLU_ALPHA * gate)))
            yacc[rows, :] += jnp.dot(act.astype(jnp.bfloat16), wdb[...],
                                     preferred_element_type=jnp.float32)

        def run(row_slices):
            gu = gate_up(row_slices[0])
            for k, rows in enumerate(row_slices):
                nxt = gate_up(row_slices[k + 1]) if k + 1 < len(row_slices) else None
                down(rows, *gu)
                gu = nxt

        n_full = nch // 2

        def quad(qd, carry):
            run([chunk_rows(4 * qd + k) for k in range(4)])
            return carry
        lax.fori_loop(0, n_full // 4, quad, 0)

        @pl.when((n_full & 2) != 0)
        def _():
            base = n_full & -4
            run([chunk_rows(base), chunk_rows(base + 1)])

        @pl.when((n_full & 1) != 0)
        def _():
            run([chunk_rows(n_full - 1)])

        @pl.when((nch & 1) != 0)
        def _():
            run([rows_at(n_full * MOE_CHUNK, MOE_HALF)])

    @pl.when((j == nf - 1) & (nch > 0))
    def _():
        def stage(rows):
            xbuf[rows, :] = yacc[rows, :].astype(xbuf.dtype)
        for_halves(stage)
        for_pieces(lambda *a: y_copy(*a).start())
        for_pieces(lambda *a: y_copy(*a).wait())


N_FFN_PREFETCH = 8


def _expert_ffn(xs, tables, w_gu, b_gu, w_down, b_down, layer):
    nf = D_FF // MOE_TF

    def wspec(block, index):
        def index_map(s, j, sb_e, sb_i0, sb_i1, sb_nch, n_sb, *_):
            return index(sb_e[s], jnp.where(s < n_sb[0], j, nf - 1))
        return pl.BlockSpec(block, index_map)

    grid_spec = pltpu.PrefetchScalarGridSpec(
        num_scalar_prefetch=N_FFN_PREFETCH,
        grid=(N_SUPER, nf),
        in_specs=[
            pl.BlockSpec(memory_space=pl.ANY),
            wspec((None, None, D_MODEL, MOE_TF), lambda e, j: (layer, e, 0, j)),
            wspec((None, None, D_MODEL, MOE_TF), lambda e, j: (layer, e, 0, nf + j)),
            wspec((None, None, MOE_TF, D_MODEL), lambda e, j: (layer, e, j, 0)),
            wspec((None, None, 1, MOE_TF), lambda e, j: (layer, e, 0, j)),
            wspec((None, None, 1, MOE_TF), lambda e, j: (layer, e, 0, nf + j)),
            wspec((None, None, 1, D_MODEL), lambda e, j: (layer, e, 0, 0)),
        ],
        out_specs=pl.BlockSpec(memory_space=pl.ANY),
        scratch_shapes=[
            pltpu.VMEM((MOE_TMAX, D_MODEL), jnp.bfloat16),
            pltpu.VMEM((MOE_TMAX, D_MODEL), jnp.float32),
            pltpu.VMEM((D_MODEL, MOE_TF), jnp.bfloat16),
            pltpu.VMEM((D_MODEL, MOE_TF), jnp.bfloat16),
            pltpu.VMEM((MOE_TF, D_MODEL), jnp.bfloat16),
            pltpu.SemaphoreType.DMA((2,)),
        ],
    )
    b_gu4 = b_gu.reshape(DEPTH, N_EXPERTS, 1, 2 * D_FF)
    return pl.pallas_call(
        _ffn_kernel,
        out_shape=jax.ShapeDtypeStruct((N_SLOTS, D_MODEL), jnp.bfloat16),
        grid_spec=grid_spec,
        input_output_aliases={N_FFN_PREFETCH: 0},
        compiler_params=pltpu.CompilerParams(dimension_semantics=("arbitrary", "arbitrary"),
                                             vmem_limit_bytes=VMEM_LIMIT, has_side_effects=True),
        name="expert_ffn",
    )(*tables, xs, w_gu, w_gu, w_down, b_gu4, b_gu4, b_down.reshape(DEPTH, N_EXPERTS, 1, D_MODEL))


def _plan_superblocks(n_pad, off):
    i32 = jnp.int32
    cur = jnp.zeros((N_EXPERTS,), i32)
    n_closed = jnp.zeros((N_EXPERTS,), i32)
    local_sb, dst = [], []
    for i in range(MOE_NT):
        n = n_pad[i]
        overflow = cur + n > MOE_TMAX
        n_closed = n_closed + overflow.astype(i32)
        cur = jnp.where(overflow, n, cur + n)
        local_sb.append(n_closed)
        dst.append(cur - n)
    local_sb = jnp.stack(local_sb)
    dst = jnp.stack(dst)
    nsb_e = jnp.where(n_pad.sum(axis=0) > 0, n_closed + 1, 0)
    sb_end = jnp.cumsum(nsb_e)
    n_sb = sb_end[-1]
    group_sb = (sb_end - nsb_e)[None, :] + local_sb
    s = jnp.arange(N_SUPER, dtype=i32)
    e_of = jnp.minimum(jnp.searchsorted(sb_end, s, side='right'), N_EXPERTS - 1).astype(i32)
    live = s < n_sb
    mine = group_sb[:, e_of] == s[None, :]
    tile = jnp.arange(MOE_NT, dtype=i32)[:, None]
    i0 = jnp.min(jnp.where(mine, tile, MOE_NT), axis=0)
    i1 = jnp.max(jnp.where(mine, tile + 1, 0), axis=0)
    rows = jnp.sum(jnp.where(mine, n_pad[:, e_of], 0), axis=0)
    last_e = e_of[jnp.maximum(n_sb - 1, 0)]
    flat = lambda a: a.T.reshape(-1).astype(i32)
    return (jnp.where(live, e_of, last_e).astype(i32),
            jnp.where(live, i0, 0).astype(i32),
            jnp.where(live, i1, 0).astype(i32),
            jnp.where(live, (rows + MOE_HALF - 1) // MOE_HALF, 0).astype(i32),
            n_sb.reshape(1).astype(i32),
            flat(tile * MOE_SMAX + off), flat(n_pad), flat(dst))


def _combine_kernel(*refs, emit_h, gate_row, shift_row, scale_row):
    if emit_h:
        x_ref, ys_ref, ls_ref, g_ref, mod_ref, modn_ref, lng_ref, lnb_ref, xo_ref, h_ref = refs
    else:
        x_ref, ys_ref, ls_ref, g_ref, mod_ref, lng_ref, lnb_ref, xo_ref = refs
    ls = ls_ref[...]
    gts = g_ref[...]
    slot = lax.broadcasted_iota(jnp.int32, (ls.shape[0], MOE_SMAX), 1)
    weights = jnp.zeros(slot.shape, jnp.float32)
    for k in range(TOP_K):
        weights = jnp.where(slot == ls[:, k:k + 1], gts[:, k:k + 1], weights)
    sub = jnp.dot(weights.astype(jnp.bfloat16), ys_ref[...], preferred_element_type=jnp.float32)
    x = _post_norm(x_ref[...], sub, mod_ref[gate_row:gate_row + 1, :], lng_ref, lnb_ref)
    xo_ref[...] = x
    if emit_h:
        h_ref[...] = _modulate(x, modn_ref, shift_row, scale_row).astype(h_ref.dtype)


def _combine_norm(x, ys, ls, gates, ada4, layer, ln_g, ln_b, *, emit_h, tile_start=0, n_tiles=MOE_NT):
    tm = 256
    per = MOE_TT // tm
    in_rows = lambda i, r: ((i + tile_start) * per + r, 0)
    row_spec = pl.BlockSpec((tm, D_MODEL), in_rows)
    small_spec = pl.BlockSpec((tm, TOP_K), in_rows)
    out_spec = pl.BlockSpec((tm, D_MODEL), lambda i, r: (i * per + r, 0))
    in_specs = [row_spec, pl.BlockSpec((MOE_SMAX, D_MODEL), lambda i, r: (i + tile_start, 0)),
                small_spec, small_spec, _ada_spec(layer, tm, per, tile_start)]
    args = [x, ys, ls, gates, ada4]
    out_shape = [jax.ShapeDtypeStruct((n_tiles * MOE_TT, D_MODEL), jnp.float32)]
    out_specs = [out_spec]
    if emit_h:
        in_specs.append(_ada_spec(layer + 1, tm, per, tile_start))
        args.append(ada4)
        out_shape.append(jax.ShapeDtypeStruct((n_tiles * MOE_TT, D_MODEL), jnp.bfloat16))
        out_specs.append(out_spec)
    ln_specs, ln_arrs = _ln_args(ln_g, ln_b, layer, 1)
    kern = functools.partial(_combine_kernel, emit_h=emit_h, gate_row=5, shift_row=0, scale_row=1)
    return pl.pallas_call(
        kern, out_shape=out_shape, grid=(n_tiles, per), in_specs=in_specs + ln_specs,
        out_specs=out_specs, compiler_params=_cparams(("arbitrary", "arbitrary")), name="combine",
    )(*args, *ln_arrs)


def kernel(x_prompt, x_sample, cache_diff_k, cache_diff_v, cache_win_k, cache_win_v, c, c_ctx, w_ada, b_ada, ln_g, ln_b, diff_wq, diff_wk, diff_wv, diff_wo, diff_lambda, diff_subln_g, win_wq, win_wk, win_wv, win_wo, win_sink, pool_w, pool_b, pool_scale, moe_w_router, moe_b_router, moe_w_gu, moe_b_gu, moe_w_down, moe_b_down):
    f32, bf16 = jnp.float32, jnp.bfloat16
    x = (x_prompt.reshape(T_P, D_MODEL), x_sample.reshape(T_S, D_MODEL))
    cvec = jnp.concatenate([c_ctx[None, :], c, jnp.zeros((COND_ROWS - N_COND, D_MODEL), f32)], axis=0)
    ada4 = _ada_all(cvec, w_ada, b_ada).reshape(DEPTH, COND_ROWS, N_ADA, D_MODEL)
    rope_a = _rope_tables(DK_A)
    rope_b = _rope_tables(DH_B)
    n_la = cache_diff_k.shape[1]
    n_lb = cache_win_k.shape[1]
    cdk = cache_diff_k.reshape(DEC_BATCH, n_la, PAST_LEN, D_MODEL)
    cdv = cache_diff_v.reshape(DEC_BATCH, n_la, PAST_LEN, D_MODEL)
    cwk = cache_win_k.reshape(DEC_BATCH, n_lb, PAST_LEN, KV_B * DH_B)
    cwv = cache_win_v.reshape(DEC_BATCH, n_lb, PAST_LEN, KV_B * DH_B)

    (h,) = _modnorm(x, None, ada4, 0, ln_g, ln_b, h_dtype=bf16, shift_row=0, scale_row=1)
    st_dk, st_dv, st_wk, st_wv = [], [], [], []
    ia = ib = ic = 0
    for i in range(DEPTH):
        kind = i % N_MIXERS
        if kind == 0:
            lam_init = 0.8 - 0.6 * math.exp(-0.3 * i)
            pj = functools.partial(_matmul, h, layer=ia)
            q_p = pj(diff_wq, row_start=0, n_rows=T_P, out_dtype=bf16)
            k_p = pj(diff_wk, row_start=0, n_rows=T_P, out_dtype=f32)
            v_p = pj(diff_wv, row_start=0, n_rows=T_P, out_dtype=f32)
            q_s = pj(diff_wq, row_start=T_P, n_rows=T_S, out_dtype=bf16, rope=rope_a)
            k_s = pj(diff_wk, row_start=T_P, n_rows=T_S, out_dtype=bf16, rope=rope_a)
            v_s = pj(diff_wv, row_start=T_P, n_rows=T_S, out_dtype=bf16)
            seq_spec = pl.BlockSpec((SEQ, DV_A), lambda b, hh, qi: (b, hh))
            o = _diff_attention(q_p, [(k_p, seq_spec)], [(v_p, seq_spec)], diff_lambda, diff_subln_g,
                                ia, lam_init, n_batch=BATCH, n_q=SEQ, tq=SEQ, out_row_start=0,
                                prev_out=None)
            cache_spec = pl.BlockSpec((None, None, PAST_LEN, DV_A), lambda b, hh, qi, ia=ia: (b, ia, 0, hh))
            lat_spec = pl.BlockSpec((DEC_SEQ, DV_A), lambda b, hh, qi: (b, hh))
            o = _diff_attention(q_s, [(cdk, cache_spec), (k_s, lat_spec)],
                                [(cdv, cache_spec), (v_s, lat_spec)], diff_lambda, diff_subln_g,
                                ia, lam_init, n_batch=DEC_BATCH, n_q=DEC_SEQ, tq=256,
                                out_row_start=T_P, prev_out=o)
            sub = _matmul(o, diff_wo, ia, row_start=0, n_rows=T_ALL, out_dtype=f32)
            st_dk.append(k_p.reshape(BATCH, SEQ, H_A, 2, DK_A))
            st_dv.append(v_p.reshape(BATCH, SEQ, H_A, DV_A))
            ia += 1
        elif kind == 1:
            pj = functools.partial(_matmul, h, layer=ib)
            q_p = pj(win_wq, row_start=0, n_rows=T_P, out_dtype=bf16)
            k_p = pj(win_wk, row_start=0, n_rows=T_P, out_dtype=f32)
            v_p = pj(win_wv, row_start=0, n_rows=T_P, out_dtype=f32)
            q_s = pj(win_wq, row_start=T_P, n_rows=T_S, out_dtype=bf16, rope=rope_b)
            k_s = pj(win_wk, row_start=T_P, n_rows=T_S, out_dtype=bf16, rope=rope_b)
            v_s = pj(win_wv, row_start=T_P, n_rows=T_S, out_dtype=bf16)
            kvw = KV_B * DH_B
            seq_spec = pl.BlockSpec((SEQ, kvw), lambda b, qi: (b, 0))
            o = _gqa_attention(q_p, [(k_p, seq_spec)], [(v_p, seq_spec)], win_sink, ib,
                               n_batch=BATCH, n_q=SEQ, tq=SEQ, band=0, out_row_start=0, prev_out=None)
            cache_spec = pl.BlockSpec((None, None, PAST_LEN, kvw), lambda b, qi, ib=ib: (b, ib, 0, 0))
            lat_spec = pl.BlockSpec((DEC_SEQ, kvw), lambda b, qi: (b, 0))
            o = _gqa_attention(q_s, [(cwk, cache_spec), (k_s, lat_spec)],
                               [(cwv, cache_spec), (v_s, lat_spec)], win_sink, ib,
                               n_batch=DEC_BATCH, n_q=DEC_SEQ, tq=WINDOW, band=3 * WINDOW,
                               out_row_start=T_P, prev_out=o)
            sub = _matmul(o, win_wo, ib, row_start=0, n_rows=T_ALL, out_dtype=f32)
            st_wk.append(k_p.reshape(BATCH, SEQ, KV_B, DH_B))
            st_wv.append(v_p.reshape(BATCH, SEQ, KV_B, DH_B))
            ib += 1
        else:
            sub = _pool_mixer(h, pool_w, pool_b, pool_scale, ic)
            ic += 1
        x, h = _modnorm(x, sub, ada4, i, ln_g, ln_b, h_dtype=f32, gate_row=2, shift_row=3, scale_row=4)
        gates, ls, n_pad, off, xs = _route(h, moe_w_router, moe_b_router, i)
        ys = _expert_ffn(xs, _plan_superblocks(n_pad, off), moe_w_gu, moe_b_gu, moe_w_down,
                         moe_b_down, i)
        if i < DEPTH - 1:
            x, h = _combine_norm(x, ys, ls, gates, ada4, i, ln_g, ln_b, emit_h=True)
    tiles_p = T_P // MOE_TT
    last = functools.partial(_combine_norm, x, ys, ls, gates, ada4, DEPTH - 1, ln_g, ln_b, emit_h=False)
    (y_p,) = last(tile_start=0, n_tiles=tiles_p)
    (y_s,) = last(tile_start=tiles_p, n_tiles=MOE_NT - tiles_p)
    y_prompt = y_p.reshape(BATCH, SEQ, D_MODEL)
    y_sample = y_s.reshape(DEC_BATCH, DEC_SEQ, D_MODEL)
    return (y_prompt, y_sample, jnp.stack(st_dk, axis=1), jnp.stack(st_dv, axis=1),
            jnp.stack(st_wk, axis=1), jnp.stack(st_wv, axis=1))
```
